```python
import math
import jax, jax.numpy as jnp
from jax import lax
import numpy as np

D_MODEL = 1024
BATCH = 16
SEQ = 2048
DEPTH = 1
DEC_BATCH = 32
DEC_SEQ = 1
PAST_LEN = 16384
PAGE_SIZE = 128

N_META = 16
Q_BLOCK = 128
FOX_HEADS = 8
FOX_DH = 64
FOX_WIDTH = FOX_HEADS * FOX_DH
DIFF_HEADS = 4
DIFF_DH = 64
DIFF_VH = 2 * DIFF_DH
DIFF_WIDTH = DIFF_HEADS * DIFF_VH
MIX_WIDTH = FOX_WIDTH + DIFF_WIDTH
IN_SPLITS = (FOX_WIDTH, FOX_WIDTH, FOX_WIDTH, FOX_HEADS,
             2 * DIFF_HEADS * DIFF_DH, 2 * DIFF_HEADS * DIFF_DH, DIFF_WIDTH)
IN_WIDTH = sum(IN_SPLITS)
FOX_SCALE = FOX_DH ** -0.5
DIFF_SCALE = DIFF_DH ** -0.5
ROPE_THETA = 10000.0
RMS_EPS = 1e-6
PEER_HEADS = 8
PEER_KEYS = 128
PEER_EXPERTS = PEER_KEYS * PEER_KEYS
PEER_TOPK = 16
PEER_QDIM = 256
PEER_HALF = PEER_QDIM // 2
PEER_BLOCK = 128

kernel_name = 'fox_diff_peer_hybrid_step'


def rmsnorm(x, g):
    x32 = x.astype(jnp.float32)
    y = x32 * lax.rsqrt(jnp.mean(x32 * x32, axis=-1, keepdims=True) + RMS_EPS)
    return (y * g.astype(jnp.float32)).astype(x.dtype)


def rope(x, pos):
    half = x.shape[-1] // 2
    inv = ROPE_THETA ** (-jnp.arange(half, dtype=jnp.float32) / half)
    ang = pos.astype(jnp.float32)[:, None] * inv[None, :]
    cos = jnp.cos(ang)[None, :, None, :]
    sin = jnp.sin(ang)[None, :, None, :]
    x32 = x.astype(jnp.float32)
    x1, x2 = x32[..., :half], x32[..., half:]
    return jnp.concatenate([x1 * cos - x2 * sin, x2 * cos + x1 * sin], axis=-1).astype(x.dtype)


def project(h, w_in, b_f, pos):
    n, L, _ = h.shape
    cuts = [int(c) for c in np.cumsum(IN_SPLITS)[:-1]]
    fq, fk, fv, ff, dq, dk, dv = jnp.split(h @ w_in, cuts, axis=-1)
    fq = fq.reshape(n, L, FOX_HEADS, FOX_DH)
    fk = fk.reshape(n, L, FOX_HEADS, FOX_DH)
    fv = fv.reshape(n, L, FOX_HEADS, FOX_DH)
    logf = jax.nn.log_sigmoid(ff.astype(jnp.float32) + b_f.astype(jnp.float32))
    dq = rope(dq.reshape(n, L, 2 * DIFF_HEADS, DIFF_DH), pos)
    dk = rope(dk.reshape(n, L, 2 * DIFF_HEADS, DIFF_DH), pos)
    dv = dv.reshape(n, L, DIFF_HEADS, DIFF_VH)
    return fq, fk, fv, logf, dq, dk, dv


def lambda_full(lq1, lk1, lq2, lk2, lam_init):
    f32 = lambda a: a.astype(jnp.float32)
    return jnp.exp(jnp.sum(f32(lq1) * f32(lk1))) - jnp.exp(jnp.sum(f32(lq2) * f32(lk2))) + lam_init


def mix_queries(fq, dq, cum_q, qpos, segs, lam, lam_init, g_fox, g_diff):
    n, q_len = fq.shape[:2]
    cq = jnp.swapaxes(cum_q, 1, 2)[..., :, None]
    fox_s, diff_s, lens = [], [], []
    for fk, fv, ck, dk, dv, kpos in segs:
        mask = kpos[None, :] <= qpos[:, None]
        s_f = (jnp.einsum('bqhd,bkhd->bhqk', fq, fk).astype(jnp.float32) * FOX_SCALE
               + (cq - jnp.swapaxes(ck, 1, 2)[..., None, :]))
        fox_s.append(jnp.where(mask, s_f, -jnp.inf))
        s_d = jnp.einsum('bqhd,bkhd->bhqk', dq, dk).astype(jnp.float32) * DIFF_SCALE
        diff_s.append(jnp.where(mask, s_d, -jnp.inf))
        lens.append(fk.shape[1])
    p_f = jax.nn.softmax(jnp.concatenate(fox_s, axis=-1), axis=-1)
    p_d = jax.nn.softmax(jnp.concatenate(diff_s, axis=-1), axis=-1)
    p_d = p_d.reshape(n, DIFF_HEADS, 2, q_len, -1)
    p_d = p_d[:, :, 0] - lam * p_d[:, :, 1]
    o_f, o_d, start = 0, 0, 0
    for (fk, fv, ck, dk, dv, kpos), k_len in zip(segs, lens):
        sl = slice(start, start + k_len)
        o_f = o_f + jnp.einsum('bhqk,bkhd->bqhd', p_f[..., sl].astype(fv.dtype), fv)
        o_d = o_d + jnp.einsum('bhqk,bkhe->bqhe', p_d[..., sl].astype(dv.dtype), dv)
        start += k_len
    o_f = rmsnorm(o_f.reshape(n, q_len, FOX_WIDTH), g_fox)
    o_d = (rmsnorm(o_d, g_diff) * (1.0 - lam_init)).reshape(n, q_len, DIFF_WIDTH)
    return jnp.concatenate([o_f, o_d.astype(o_f.dtype)], axis=-1)


def peer_ffn(h, w_pq, sub_k1, sub_k2, expert_u, expert_v):
    n, L, d = h.shape
    t_len = n * L
    n_blk = -(-t_len // PEER_BLOCK)
    t = jnp.pad(h.reshape(t_len, d), ((0, n_blk * PEER_BLOCK - t_len), (0, 0)))
    t = t.reshape(n_blk, PEER_BLOCK, d)
    k1 = sub_k1.astype(jnp.float32)
    k2 = sub_k2.astype(jnp.float32)

    def block(xb):
        q = (xb @ w_pq).astype(jnp.float32).reshape(PEER_BLOCK, PEER_HEADS, 2, PEER_HALF)
        s1 = jnp.einsum('thc,kc->thk', q[:, :, 0], k1)
        s2 = jnp.einsum('thc,kc->thk', q[:, :, 1], k2)
        v1, i1 = lax.top_k(s1, PEER_TOPK)
        v2, i2 = lax.top_k(s2, PEER_TOPK)
        cand = (v1[..., :, None] + v2[..., None, :]).reshape(PEER_BLOCK, PEER_HEADS, PEER_TOPK * PEER_TOPK)
        cidx = (i1[..., :, None] * PEER_KEYS + i2[..., None, :]).reshape(PEER_BLOCK, PEER_HEADS, PEER_TOPK * PEER_TOPK)
        sc, sel = lax.top_k(cand, PEER_TOPK)
        idx = jnp.take_along_axis(cidx, sel, axis=-1)
        g = jax.nn.softmax(sc, axis=-1)
        u = expert_u[idx]
        a = jax.nn.gelu(jnp.einsum('td,thkd->thk', xb, u).astype(jnp.float32), approximate=False)
        return jnp.einsum('thk,thkd->td', (g * a).astype(xb.dtype), expert_v[idx])

    out = lax.map(block, t).reshape(n_blk * PEER_BLOCK, d)[:t_len]
    return out.reshape(n, L, d)


def setup_inputs(seed: int = 0) -> dict:
    key = jax.random.key(seed)
    ks = jax.random.split(key, 26)
    n_pages = PAST_LEN // PAGE_SIZE
    n_used = DEC_BATCH * n_pages
    n_pool = n_used + n_used // 4
    nrm = lambda k, shape, s=1.0: jax.random.normal(k, shape, jnp.float32) * s
    page_table = jax.random.permutation(ks[7], n_pool)[:n_used].reshape(DEC_BATCH, n_pages).astype(jnp.int32)
    return {
        'x_prompt': nrm(ks[0], (BATCH, SEQ, D_MODEL)),
        'x_sample': nrm(ks[1], (DEC_BATCH, DEC_SEQ, D_MODEL)),
        'cache_fox_k': nrm(ks[2], (DEPTH, n_pool, PAGE_SIZE, FOX_HEADS, FOX_DH)),
        'cache_fox_v': nrm(ks[3], (DEPTH, n_pool, PAGE_SIZE, FOX_HEADS, FOX_DH)),
        'cache_fox_logf': jax.nn.log_sigmoid(3.0 + nrm(ks[4], (DEPTH, n_pool, PAGE_SIZE, FOX_HEADS))),
        'cache_diff_k': nrm(ks[5], (DEPTH, n_pool, PAGE_SIZE, DIFF_HEADS, DIFF_VH)),
        'cache_diff_v': nrm(ks[6], (DEPTH, n_pool, PAGE_SIZE, DIFF_HEADS, DIFF_VH)),
        'page_table': page_table,
        'meta_tokens': nrm(ks[8], (N_META, D_MODEL)),
        'g_attn': 1.0 + nrm(ks[9], (DEPTH, D_MODEL), 0.01),
        'w_in': nrm(ks[10], (DEPTH, D_MODEL, IN_WIDTH), D_MODEL ** -0.5),
        'b_f': 3.0 + nrm(ks[11], (DEPTH, FOX_HEADS), 0.1),
        'lam_q1': nrm(ks[12], (DEPTH, DIFF_DH), 0.1),
        'lam_k1': nrm(ks[13], (DEPTH, DIFF_DH), 0.1),
        'lam_q2': nrm(ks[14], (DEPTH, DIFF_DH), 0.1),
        'lam_k2': nrm(ks[15], (DEPTH, DIFF_DH), 0.1),
        'g_fox': 1.0 + nrm(ks[16], (DEPTH, FOX_WIDTH), 0.01),
        'g_diff': 1.0 + nrm(ks[17], (DEPTH, DIFF_VH), 0.01),
        'w_o': nrm(ks[18], (DEPTH, MIX_WIDTH, D_MODEL), MIX_WIDTH ** -0.5),
        'g_ffn': 1.0 + nrm(ks[19], (DEPTH, D_MODEL), 0.01),
        'w_pq': nrm(ks[20], (DEPTH, D_MODEL, PEER_HEADS * PEER_QDIM), D_MODEL ** -0.5),
        'sub_k1': nrm(ks[21], (DEPTH, PEER_KEYS, PEER_HALF), PEER_HALF ** -0.5),
        'sub_k2': nrm(ks[22], (DEPTH, PEER_KEYS, PEER_HALF), PEER_HALF ** -0.5),
        'expert_u': nrm(ks[23], (DEPTH, PEER_EXPERTS, D_MODEL), D_MODEL ** -0.5),
        'expert_v': nrm(ks[24], (DEPTH, PEER_EXPERTS, D_MODEL), 0.5),
        'g_final': 1.0 + nrm(ks[25], (D_MODEL,), 0.01),
    }


def reference(x_prompt, x_sample, cache_fox_k, cache_fox_v, cache_fox_logf, cache_diff_k,
              cache_diff_v, page_table, meta_tokens, g_attn, w_in, b_f, lam_q1, lam_k1,
              lam_q2, lam_k2, g_fox, g_diff, w_o, g_ffn, w_pq, sub_k1, sub_k2, expert_u,
              expert_v, g_final):
    b = x_prompt.shape[0]
    db, s_new = x_sample.shape[:2]
    n_pages = page_table.shape[1]
    past = n_pages * PAGE_SIZE
    xp = jnp.concatenate([jnp.broadcast_to(meta_tokens.astype(x_prompt.dtype)[None], (b, N_META, D_MODEL)),
                          x_prompt], axis=1)
    L = xp.shape[1]
    n_blk = (L - N_META) // Q_BLOCK
    pos_p = jnp.arange(L)
    pos_s = past + jnp.arange(s_new)
    xs = x_sample
    pfk, pfv, pfl, pdk, pdv = [], [], [], [], []
    sfk, sfv, sfl, sdk, sdv = [], [], [], [], []
    for l in range(DEPTH):
        lam_init = 0.8 - 0.6 * math.exp(-0.3 * l)
        lam = lambda_full(lam_q1[l], lam_k1[l], lam_q2[l], lam_k2[l], lam_init)

        h = rmsnorm(xp, g_attn[l])
        fq, fk, fv, logf, dq, dk, dv = project(h, w_in[l], b_f[l], pos_p)
        cum = lax.cumsum(logf, axis=1)
        segs = ((fk, fv, cum, dk, dv, pos_p),)
        mix_fn = lambda q_f, q_d, c_q, q_p: mix_queries(q_f, q_d, c_q, q_p, segs, lam, lam_init, g_fox[l], g_diff[l])
        meta_out = mix_fn(fq[:, :N_META], dq[:, :N_META], cum[:, :N_META], pos_p[:N_META])
        to_blocks = lambda a: jnp.swapaxes(a[:, N_META:].reshape((b, n_blk, Q_BLOCK) + a.shape[2:]), 0, 1)
        blk_out = lax.map(lambda args: mix_fn(*args),
                          (to_blocks(fq), to_blocks(dq), to_blocks(cum), pos_p[N_META:].reshape(n_blk, Q_BLOCK)))
        blk_out = jnp.swapaxes(blk_out, 0, 1).reshape(b, n_blk * Q_BLOCK, MIX_WIDTH)
        mixed = jnp.concatenate([meta_out, blk_out], axis=1)
        xp = xp + mixed @ w_o[l]
        xp = xp + peer_ffn(rmsnorm(xp, g_ffn[l]), w_pq[l], sub_k1[l], sub_k2[l], expert_u[l], expert_v[l])
        pfk.append(fk)
        pfv.append(fv)
        pfl.append(logf)
        pdk.append(dk.reshape(b, L, DIFF_HEADS, DIFF_VH))
        pdv.append(dv)

        h = rmsnorm(xs, g_attn[l])
        qf_s, kf_s, vf_s, logf_s, qd_s, kd_s, vd_s = project(h, w_in[l], b_f[l], pos_s)
        gather = lambda c: c[l, page_table].reshape((db, past) + c.shape[3:])
        kf_p = gather(cache_fox_k)
        vf_p = gather(cache_fox_v)
        logf_p = gather(cache_fox_logf).astype(jnp.float32)
        kd_p = gather(cache_diff_k).reshape(db, past, 2 * DIFF_HEADS, DIFF_DH)
        vd_p = gather(cache_diff_v)
        logf_all = jnp.concatenate([logf_p, logf_s], axis=1)
        cum_all = logf_all - lax.cumsum(logf_all, axis=1, reverse=True)
        segs_s = ((kf_p, vf_p, cum_all[:, :past], kd_p, vd_p, jnp.arange(past)),
                  (kf_s, vf_s, cum_all[:, past:], kd_s, vd_s, pos_s))
        mixed_s = mix_queries(qf_s, qd_s, cum_all[:, past:], pos_s, segs_s, lam, lam_init, g_fox[l], g_diff[l])
        xs = xs + mixed_s @ w_o[l]
        xs = xs + peer_ffn(rmsnorm(xs, g_ffn[l]), w_pq[l], sub_k1[l], sub_k2[l], expert_u[l], expert_v[l])
        sfk.append(kf_s)
        sfv.append(vf_s)
        sfl.append(logf_s)
        sdk.append(kd_s.reshape(db, s_new, DIFF_HEADS, DIFF_VH))
        sdv.append(vd_s)

    y_prompt = rmsnorm(xp, g_final)[:, N_META:]
    y_sample = rmsnorm(xs, g_final)
    p_fox_k = jnp.stack(pfk, 0)
    p_fox_v = jnp.stack(pfv, 0)
    p_fox_logf = jnp.stack(pfl, 0)
    p_diff_k = jnp.stack(pdk, 0)
    p_diff_v = jnp.stack(pdv, 0)
    s_fox_k = jnp.stack(sfk, 0)
    s_fox_v = jnp.stack(sfv, 0)
    s_fox_logf = jnp.stack(sfl, 0)
    s_diff_k = jnp.stack(sdk, 0)
    s_diff_v = jnp.stack(sdv, 0)
    return (y_prompt, y_sample, p_fox_k, p_fox_v, p_fox_logf, p_diff_k, p_diff_v,
            s_fox_k, s_fox_v, s_fox_logf, s_diff_k, s_diff_v)
```

```python
import functools
import math

import jax
import jax.numpy as jnp
from jax import lax
from jax.experimental import pallas as pl
from jax.experimental.pallas import tpu as pltpu

F32 = jnp.float32
BF16 = jnp.bfloat16
I32 = jnp.int32

D_MODEL = 1024
N_META = 16
PAGE = 128
FOX_HEADS = 8
HEAD_DIM = 64
GROUP_W = 512
MIX_W = 1024
ROPE_THETA = 10000.0
RMS_EPS = 1e-6
QK_SCALE = 0.125
LAM_INIT = 0.2
PEER_HEADS = 8
PEER_KEYS = 128
PEER_TOPK = 16
PEER_PICKS = PEER_HEADS * PEER_TOPK
LANES = 128
Q_TILE = 128
NEG_INF = float("-inf")
W_IN_COLS = 6 * GROUP_W + LANES

VMEM_LIMIT = 56 * 1024 * 1024


def _params(n_grid, vmem=VMEM_LIMIT):
    return pltpu.CompilerParams(dimension_semantics=("arbitrary",) * n_grid,
                                vmem_limit_bytes=vmem)


def _rms(x, g):
    return x * lax.rsqrt(jnp.mean(x * x, axis=-1, keepdims=True) + RMS_EPS) * g


def _dot_t(a, b):
    return lax.dot_general(a, b, (((1,), (1,)), ((), ())), preferred_element_type=F32)


def _dot(a, b):
    return jnp.dot(a, b, preferred_element_type=F32)


def _inproj_kernel(x_ref, g_ref, w_ref, bf_ref, cos_ref, sa_ref, sb_ref,
                   fq16, fk32, fk16, fv32, fv16, dq16, dk32, dk16, dv32, dv16, lf_ref):
    x = x_ref[0]
    h16 = _rms(x, g_ref[...]).astype(BF16)

    def seg(k, width=GROUP_W):
        return _dot(h16, w_ref[:, GROUP_W * k:GROUP_W * k + width])

    cos = jnp.concatenate([cos_ref[...]] * 4, axis=1)
    sa = jnp.concatenate([sa_ref[...]] * 4, axis=1)
    sb = jnp.concatenate([sb_ref[...]] * 4, axis=1)

    def rope(v):
        return (v * cos + pltpu.roll(v, GROUP_W - HEAD_DIM // 2, axis=1) * sa
                + pltpu.roll(v, HEAD_DIM // 2, axis=1) * sb)

    fq = seg(0)
    fq16[0] = (fq * QK_SCALE).astype(BF16)
    fk = seg(1)
    fk32[0] = fk
    fk16[0] = fk.astype(BF16)
    fv = seg(2)
    fv32[0] = fv
    fv16[0] = fv.astype(BF16)
    dq = rope(seg(3))
    dq16[0] = (dq * QK_SCALE).astype(BF16)
    dk = rope(seg(4))
    dk32[0] = dk
    dk16[0] = dk.astype(BF16)
    dv = seg(5)
    dv32[0] = dv
    dv16[0] = dv.astype(BF16)
    ff = seg(6, LANES) + bf_ref[...]
    lf_ref[0] = (jnp.minimum(ff, 0.0) - jnp.log1p(jnp.exp(-jnp.abs(ff))))[:, :FOX_HEADS]


def _inproj(x, g_attn, w_re, bf_pad, cos, sa, sb, rows_out, tm):
    n, rows_in, _ = x.shape
    grid = (n, rows_in // tm)
    row_blk = lambda w: pl.BlockSpec((1, tm, w), lambda b, j: (b, j, 0))
    const = lambda shape: pl.BlockSpec(shape, lambda b, j: (0,) * len(shape))
    tab = pl.BlockSpec((tm, LANES), lambda b, j: (j, 0))
    o16 = jax.ShapeDtypeStruct((n, rows_in, GROUP_W), BF16)
    o32 = jax.ShapeDtypeStruct((n, rows_out, GROUP_W), F32)
    return pl.pallas_call(
        _inproj_kernel,
        grid=grid,
        in_specs=[row_blk(D_MODEL), const((1, D_MODEL)), const((D_MODEL, W_IN_COLS)),
                  const((1, LANES)), tab, tab, tab],
        out_specs=[row_blk(GROUP_W)] * 10 + [row_blk(FOX_HEADS)],
        out_shape=[o16, o32, o16, o32, o16, o16, o32, o16, o32, o16,
                   jax.ShapeDtypeStruct((n, rows_out, FOX_HEADS), F32)],
        compiler_params=_params(2),
        name="inproj",
    )(x, g_attn, w_re, bf_pad, cos, sa, sb)


def _cumsum_kernel(x_ref, o_ref):
    rows, cols = x_ref.shape
    lane = lax.broadcasted_iota(I32, (rows, LANES), 1)
    carry = jnp.zeros((rows, 1), F32)
    for c in range(cols // LANES):
        x = x_ref[:, LANES * c:LANES * (c + 1)]
        s = 1
        while s < LANES:
            x = x + jnp.where(lane >= s, pltpu.roll(x, s, axis=1), 0.0)
            s *= 2
        x = x + carry
        o_ref[:, LANES * c:LANES * (c + 1)] = x
        carry = x[:, LANES - 1:LANES]


def _cumsum_lanes(x):
    return pl.pallas_call(
        _cumsum_kernel,
        out_shape=jax.ShapeDtypeStruct(x.shape, F32),
        name="cumsum",
    )(x)


def _lambda(lq1, lk1, lq2, lk2):
    return (jnp.exp(jnp.sum(lq1[...] * lk1[...], axis=1, keepdims=True))
            - jnp.exp(jnp.sum(lq2[...] * lk2[...], axis=1, keepdims=True)) + LAM_INIT)


def _attn_kernel(fq_ref, dq_ref, fk_ref, fv_ref, dk_ref, dv_ref, cq_ref, ck_ref,
                 lq1, lk1, lq2, lk2, gfox_ref, gdiff_ref, o_ref):
    t = pl.program_id(1)
    rows2 = 2 * Q_TILE
    row = lax.broadcasted_iota(I32, (rows2, LANES), 0)
    col = lax.broadcasted_iota(I32, (rows2, LANES), 1)
    top = row < Q_TILE
    qpos = jnp.where(top, row, row - Q_TILE)
    lane_lo = lax.broadcasted_iota(I32, (Q_TILE, LANES), 1) < HEAD_DIM

    def stack(q):
        q = q.astype(F32)
        return jnp.concatenate([jnp.where(lane_lo, q, 0.0), jnp.where(lane_lo, 0.0, q)],
                               axis=0).astype(BF16)

    def run(qs, k_ref, v_ref, g, bias_fn):
        def body(j, carry):
            m, l, acc = carry
            start = pl.multiple_of(j * Q_TILE, Q_TILE)
            k = k_ref[0, pl.ds(start, Q_TILE), LANES * g:LANES * (g + 1)]
            v = v_ref[0, pl.ds(start, Q_TILE), LANES * g:LANES * (g + 1)]
            s = _dot_t(qs, k)
            if bias_fn is not None:
                s = s + bias_fn(start)
            s = jnp.where(col <= qpos + jnp.where(j < t, Q_TILE, 0), s, NEG_INF)
            m_new = jnp.maximum(m, jnp.max(s, axis=1, keepdims=True))
            alpha = jnp.exp(m - m_new)
            p = jnp.exp(s - m_new)
            l = alpha * l + jnp.sum(p, axis=1, keepdims=True)
            acc = alpha * acc + _dot(p.astype(BF16), v)
            return m_new, l, acc

        init = (jnp.full((rows2, 1), NEG_INF, F32), jnp.zeros((rows2, 1), F32),
                jnp.zeros((rows2, LANES), F32))
        _, l, acc = lax.fori_loop(0, t + 1, body, init)
        return acc / l

    fox = []
    for g in range(4):
        qs = stack(fq_ref[0, :, LANES * g:LANES * (g + 1)])
        cq = jnp.concatenate([cq_ref[0, :, 2 * g:2 * g + 1], cq_ref[0, :, 2 * g + 1:2 * g + 2]], axis=0)

        def bias(start, g=g, cq=cq):
            ck = ck_ref[0, 2 * g:2 * g + 2, pl.ds(start, Q_TILE)]
            return cq - jnp.where(top, ck[0:1], ck[1:2])

        o = run(qs, fk_ref, fv_ref, g, bias)
        fox.append(jnp.where(lane_lo, o[:Q_TILE], o[Q_TILE:]))
    o_f = _rms(jnp.concatenate(fox, axis=1), gfox_ref[...])

    lam = _lambda(lq1, lk1, lq2, lk2)
    diff = []
    for g in range(4):
        qs = stack(dq_ref[0, :, LANES * g:LANES * (g + 1)])
        o = run(qs, dk_ref, dv_ref, g, None)
        od = o[:Q_TILE] - lam * o[Q_TILE:]
        diff.append(_rms(od, gdiff_ref[...]) * (1.0 - LAM_INIT))
    o_ref[0] = jnp.concatenate([o_f] + diff, axis=1).astype(BF16)


def _attention(fq16, dq16, fk16, fv16, dk16, dv16, cq, ck, lams, g_fox, g_diff):
    n, lp, _ = fq16.shape
    grid = (n, lp // Q_TILE)
    tile = lambda w: pl.BlockSpec((1, Q_TILE, w), lambda b, t: (b, t, 0))
    full = lambda r, w: pl.BlockSpec((1, r, w), lambda b, t: (b, 0, 0))
    const = lambda shape: pl.BlockSpec(shape, lambda b, t: (0,) * len(shape))
    return pl.pallas_call(
        _attn_kernel,
        grid=grid,
        in_specs=[tile(GROUP_W), tile(GROUP_W), full(lp, GROUP_W), full(lp, GROUP_W),
                  full(lp, GROUP_W), full(lp, GROUP_W), tile(FOX_HEADS), full(FOX_HEADS, lp)]
                 + [const((1, HEAD_DIM))] * 4 + [const((1, GROUP_W)), const((1, LANES))],
        out_specs=tile(MIX_W),
        out_shape=jax.ShapeDtypeStruct((n, lp, MIX_W), BF16),
        compiler_params=_params(2),
        name="attention",
    )(fq16, dq16, fk16, fv16, dk16, dv16, cq, ck, *lams, g_fox, g_diff)


def _decode_kernel(pt_ref, fq_ref, dq_ref, fkn_ref, fvn_ref, dkn_ref, dvn_ref, lfn_ref,
                   ck_ref, cv_ref, clf_ref, cdk_ref, cdv_ref,
                   lq1, lk1, lq2, lk2, gfox_ref, gdiff_ref, o_ref,
                   mf, lf, af, md, ld, ad, carry):
    p = pl.program_id(1)
    n_pages = pl.num_programs(1)
    heads = FOX_HEADS
    hrow = lax.broadcasted_iota(I32, (heads, GROUP_W), 0)
    hlane = lax.broadcasted_iota(I32, (heads, GROUP_W), 1)
    own64 = (hlane // HEAD_DIM) == hrow
    own128 = (hlane // LANES) == (hrow // 2)

    def stack(q):
        return jnp.where(own64, jnp.broadcast_to(q.astype(F32), (heads, GROUP_W)), 0.0)

    qf32 = stack(fq_ref[0])
    qd32 = stack(dq_ref[0])
    qf = qf32.astype(BF16)
    qd = qd32.astype(BF16)

    @pl.when(p == 0)
    def _():
        mf[...] = jnp.sum(qf32 * fkn_ref[0].astype(F32), axis=1, keepdims=True)
        lf[...] = jnp.ones_like(lf)
        af[...] = jnp.broadcast_to(fvn_ref[0].astype(F32), (heads, GROUP_W))
        md[...] = jnp.sum(qd32 * dkn_ref[0].astype(F32), axis=1, keepdims=True)
        ld[...] = jnp.ones_like(ld)
        ad[...] = jnp.broadcast_to(dvn_ref[0].astype(F32), (heads, GROUP_W))
        carry[...] = lfn_ref[0]

    def update(m_ref, l_ref, a_ref, s, v16):
        m = m_ref[...]
        m_new = jnp.maximum(m, jnp.max(s, axis=1, keepdims=True))
        alpha = jnp.exp(m - m_new)
        pr = jnp.exp(s - m_new)
        l_ref[...] = alpha * l_ref[...] + jnp.sum(pr, axis=1, keepdims=True)
        a_ref[...] = alpha * a_ref[...] + _dot(pr.astype(BF16), v16)
        m_ref[...] = m_new

    x = clf_ref[0]
    lane = lax.broadcasted_iota(I32, (heads, PAGE), 1)
    incl = x
    s_ = 1
    while s_ < PAGE:
        incl = incl + jnp.where(lane < PAGE - s_, pltpu.roll(incl, PAGE - s_, axis=1), 0.0)
        s_ *= 2
    bias = (incl - x) + carry[...]
    carry[...] = carry[...] + incl[:, 0:1]

    update(mf, lf, af, _dot_t(qf, ck_ref[0].astype(BF16)) + bias, cv_ref[0].astype(BF16))
    update(md, ld, ad, _dot_t(qd, cdk_ref[0].astype(BF16)), cdv_ref[0].astype(BF16))

    @pl.when(p == n_pages - 1)
    def _():
        of = af[...] / lf[...]
        o_f = jnp.sum(jnp.where(own64, of, 0.0), axis=0, keepdims=True)
        o_f = _rms(o_f, gfox_ref[...])
        lam = _lambda(lq1, lk1, lq2, lk2)
        coef = jnp.where(lax.broadcasted_iota(I32, (heads, 1), 0) % 2 == 0, 1.0, -lam)
        od = ad[...] / ld[...] * coef
        o_d = jnp.sum(jnp.where(own128, od, 0.0), axis=0, keepdims=True)
        parts = [_rms(o_d[:, LANES * h:LANES * (h + 1)], gdiff_ref[...]) * (1.0 - LAM_INIT)
                 for h in range(4)]
        o_ref[0] = jnp.concatenate([o_f] + parts, axis=1).astype(BF16)


def _decode(page_table, fq16, dq16, fk16, fv16, dk16, dv16, lf_col,
            cache_k, cache_v, cache_lf_t, cache_dk, cache_dv, lams, g_fox, g_diff):
    nb, n_pages = page_table.shape
    tok = lambda w: pl.BlockSpec((1, 1, w), lambda b, p, pt: (b, 0, 0))
    page = lambda r, w: pl.BlockSpec((1, r, w), lambda b, p, pt: (pt[b, n_pages - 1 - p], 0, 0))
    const = lambda shape: pl.BlockSpec(shape, lambda b, p, pt: (0,) * len(shape))
    grid_spec = pltpu.PrefetchScalarGridSpec(
        num_scalar_prefetch=1,
        grid=(nb, n_pages),
        in_specs=[tok(GROUP_W)] * 6 + [pl.BlockSpec((1, FOX_HEADS, 1), lambda b, p, pt: (b, 0, 0)),
                  page(PAGE, GROUP_W), page(PAGE, GROUP_W), page(FOX_HEADS, PAGE),
                  page(PAGE, GROUP_W), page(PAGE, GROUP_W)]
                 + [const((1, HEAD_DIM))] * 4 + [const((1, GROUP_W)), const((1, LANES))],
        out_specs=pl.BlockSpec((1, 1, MIX_W), lambda b, p, pt: (b, 0, 0)),
        scratch_shapes=[pltpu.VMEM((FOX_HEADS, 1), F32), pltpu.VMEM((FOX_HEADS, 1), F32),
                        pltpu.VMEM((FOX_HEADS, GROUP_W), F32),
                        pltpu.VMEM((FOX_HEADS, 1), F32), pltpu.VMEM((FOX_HEADS, 1), F32),
                        pltpu.VMEM((FOX_HEADS, GROUP_W), F32),
                        pltpu.VMEM((FOX_HEADS, 1), F32)],
    )
    return pl.pallas_call(
        _decode_kernel,
        grid_spec=grid_spec,
        out_shape=jax.ShapeDtypeStruct((nb, 1, MIX_W), BF16),
        compiler_params=_params(2),
        name="decode",
    )(page_table, fq16, dq16, fk16, fv16, dk16, dv16, lf_col,
      cache_k, cache_v, cache_lf_t, cache_dk, cache_dv, *lams, g_fox, g_diff)


def _topk_rows(s, payload=None):
    n = s.shape[0]
    rows = lax.broadcasted_iota(I32, s.shape, 0)
    vals, picks = [], []
    for _ in range(PEER_TOPK):
        m = jnp.max(s, axis=0, keepdims=True)
        r = jnp.min(jnp.where(s == m, rows, n), axis=0, keepdims=True)
        hit = rows == r
        vals.append(m)
        if payload is None:
            picks.append(r)
        else:
            picks.append(jnp.sum(jnp.where(hit, payload, 0), axis=0, keepdims=True))
        s = jnp.where(hit, NEG_INF, s)
    return jnp.concatenate(vals, axis=0), jnp.concatenate(picks, axis=0)


def _mix_kernel(x_ref, mixed_ref, wo_ref, g_ref, wpq_ref, k1_ref, k2_ref,
                x2_ref, h2_ref, idx_ref, gate_ref):
    x2 = x_ref[0] + _dot(mixed_ref[0], wo_ref[...])
    x2_ref[0] = x2
    h16 = _rms(x2, g_ref[...]).astype(BF16)
    h2_ref[0] = h16.astype(F32)
    for hd in range(PEER_HEADS):
        q = _dot(h16, wpq_ref[:, 2 * LANES * hd:2 * LANES * (hd + 1)])
        s1 = _dot_t(k1_ref[...], q[:, :LANES].astype(BF16))
        s2 = _dot_t(k2_ref[...], q[:, LANES:].astype(BF16))
        v1, i1 = _topk_rows(s1)
        v2, i2 = _topk_rows(s2)
        cand = jnp.concatenate([v1[a:a + 1] + v2 for a in range(PEER_TOPK)], axis=0)
        cidx = jnp.concatenate([i1[a:a + 1] * PEER_KEYS + i2 for a in range(PEER_TOPK)], axis=0)
        sc, idx = _topk_rows(cand, cidx)
        e = jnp.exp(sc - sc[0:1])
        gate = e / jnp.sum(e, axis=0, keepdims=True)
        idx_ref[0, PEER_TOPK * hd:PEER_TOPK * (hd + 1), :] = idx
        gate_ref[0, PEER_TOPK * hd:PEER_TOPK * (hd + 1), :] = gate


def _mix(x, mixed, w_o16, g_ffn, w_pq16, k1_16, k2_16, tm):
    n, rows, _ = x.shape
    grid = (n, rows // tm)
    row_blk = lambda w: pl.BlockSpec((1, tm, w), lambda b, j: (b, j, 0))
    col_blk = pl.BlockSpec((1, PEER_PICKS, tm), lambda b, j: (b, 0, j))
    const = lambda shape: pl.BlockSpec(shape, lambda b, j: (0,) * len(shape))
    return pl.pallas_call(
        _mix_kernel,
        grid=grid,
        in_specs=[row_blk(D_MODEL), row_blk(MIX_W), const((MIX_W, D_MODEL)), const((1, D_MODEL)),
                  const((D_MODEL, 2 * LANES * PEER_HEADS)), const((PEER_KEYS, LANES)),
                  const((PEER_KEYS, LANES))],
        out_specs=[row_blk(D_MODEL), row_blk(D_MODEL), col_blk, col_blk],
        out_shape=[jax.ShapeDtypeStruct((n, rows, D_MODEL), F32),
                   jax.ShapeDtypeStruct((n, rows, D_MODEL), F32),
                   jax.ShapeDtypeStruct((n, PEER_PICKS, rows), I32),
                   jax.ShapeDtypeStruct((n, PEER_PICKS, rows), F32)],
        compiler_params=_params(2),
        name="mix",
    )(x, mixed, w_o16, g_ffn, w_pq16, k1_16, k2_16)


HALF_ROWS = 4
TOKEN_ROWS = 2 * HALF_ROWS
HI_MASK = -65536


def _unpack(word):
    hi = pltpu.bitcast(word & HI_MASK, F32)
    lo = pltpu.bitcast(word << 16, F32)
    return hi, lo


def _expert_row(tab_ref, e):
    return tab_ref[pl.ds(pl.multiple_of(e * HALF_ROWS, HALF_ROWS), HALF_ROWS), :]


def _peer_u_kernel(idx_ref, x_ref, gate_ref, tab_ref, w_ref, prod, act):
    tb = gate_ref.shape[0]

    def token(t, _):
        xt = x_ref[pl.ds(pl.multiple_of(t * TOKEN_ROWS, TOKEN_ROWS), TOKEN_ROWS), :]
        xa = xt[:HALF_ROWS]
        xb = xt[HALF_ROWS:]
        for j in range(PEER_PICKS):
            hi, lo = _unpack(_expert_row(tab_ref, idx_ref[t, j]))
            prod[j, :HALF_ROWS, :] = hi * xa + lo * xb
        part = (prod[:, 0, :] + prod[:, 1, :]) + (prod[:, 2, :] + prod[:, 3, :])
        act[pl.ds(t, 1), :] = jnp.sum(part.T, axis=0, keepdims=True)
        return 0

    lax.fori_loop(0, tb, token, 0)
    a = act[...]
    w = gate_ref[...] * (0.5 * a * (1.0 + lax.erf(a * (2.0 ** -0.5))))
    w_ref[...] = w.astype(BF16).astype(F32)


def _peer_v_kernel(idx_ref, w_ref, tab_ref, o_ref):
    tb = idx_ref.shape[0]
    n_acc = 4

    def token(t, _):
        acc_a = [jnp.zeros((HALF_ROWS, LANES), F32) for _ in range(n_acc)]
        acc_b = [jnp.zeros((HALF_ROWS, LANES), F32) for _ in range(n_acc)]
        for j in range(PEER_PICKS):
            hi, lo = _unpack(_expert_row(tab_ref, idx_ref[t, j]))
            w = w_ref[t, j]
            acc_a[j % n_acc] = acc_a[j % n_acc] + w * hi
            acc_b[j % n_acc] = acc_b[j % n_acc] + w * lo
        a = (acc_a[0] + acc_a[1]) + (acc_a[2] + acc_a[3])
        b = (acc_b[0] + acc_b[1]) + (acc_b[2] + acc_b[3])
        o_ref[pl.ds(pl.multiple_of(t * TOKEN_ROWS, TOKEN_ROWS), TOKEN_ROWS), :] = (
            jnp.concatenate([a, b], axis=0))
        return 0

    lax.fori_loop(0, tb, token, 0)


def _resident(shape, n_grid):
    return pl.BlockSpec(shape, lambda *_: (0,) * len(shape), pipeline_mode=pl.Buffered(1))


def _peer_u(idx, h2_tiles, gate, table, tb):
    n, rows, _ = idx.shape
    grid = (n, rows // tb)
    return pl.pallas_call(
        _peer_u_kernel,
        grid=grid,
        in_specs=[pl.BlockSpec((None, tb, PEER_PICKS), lambda b, j: (b, j, 0),
                               memory_space=pltpu.SMEM),
                  pl.BlockSpec((None, tb * TOKEN_ROWS, LANES), lambda b, j: (b, j, 0)),
                  pl.BlockSpec((None, tb, PEER_PICKS), lambda b, j: (b, j, 0)),
                  _resident(table.shape, 2)],
        out_specs=pl.BlockSpec((None, tb, PEER_PICKS), lambda b, j: (b, j, 0)),
        out_shape=jax.ShapeDtypeStruct((n, rows, PEER_PICKS), F32),
        scratch_shapes=[pltpu.VMEM((PEER_PICKS, HALF_ROWS, LANES), F32),
                        pltpu.VMEM((tb, PEER_PICKS), F32)],
        compiler_params=_params(2),
        name="peer_u",
    )(idx, h2_tiles, gate, table)


def _peer_v(idx, w, table, tb):
    n, rows, _ = idx.shape
    grid = (n, rows // tb)
    smem = lambda: pl.BlockSpec((None, tb, PEER_PICKS), lambda b, j: (b, j, 0),
                                memory_space=pltpu.SMEM)
    return pl.pallas_call(
        _peer_v_kernel,
        grid=grid,
        in_specs=[smem(), smem(), _resident(table.shape, 2)],
        out_specs=pl.BlockSpec((None, tb * TOKEN_ROWS, LANES), lambda b, j: (b, j, 0)),
        out_shape=jax.ShapeDtypeStruct((n, rows * TOKEN_ROWS, LANES), F32),
        compiler_params=_params(2),
        name="peer_v",
    )(idx, w, table)


def _pack_table(t):
    t16 = t.astype(BF16)
    hi = lax.bitcast_convert_type(t16[:, :GROUP_W], jnp.uint16).astype(jnp.uint32)
    lo = lax.bitcast_convert_type(t16[:, GROUP_W:], jnp.uint16).astype(jnp.uint32)
    packed = lax.bitcast_convert_type((hi << 16) | lo, I32)
    return packed.reshape(t.shape[0] * HALF_ROWS, LANES)


def _final_kernel(x_ref, p_ref, g_ref, y_ref):
    y_ref[0] = _rms(x_ref[0] + p_ref[0], g_ref[...])


def _final(x2, peer, g_final, tm):
    n, rows, _ = peer.shape
    grid = (n, rows // tm)
    blk = pl.BlockSpec((1, tm, D_MODEL), lambda b, j: (b, j, 0))
    return pl.pallas_call(
        _final_kernel,
        grid=grid,
        in_specs=[blk, blk, pl.BlockSpec((1, D_MODEL), lambda b, j: (0, 0))],
        out_specs=blk,
        out_shape=jax.ShapeDtypeStruct((n, rows, D_MODEL), F32),
        compiler_params=_params(2),
        name="final",
    )(x2, peer, g_final)


def _rope_tables(pos):
    half = HEAD_DIM // 2
    inv = ROPE_THETA ** (-jnp.arange(half, dtype=F32) / half)
    ang = pos.astype(F32)[:, None] * inv[None, :]
    cos, sin = jnp.cos(ang), jnp.sin(ang)
    zero = jnp.zeros_like(sin)
    return (jnp.concatenate([cos, cos, cos, cos], axis=1),
            jnp.concatenate([-sin, zero, -sin, zero], axis=1),
            jnp.concatenate([zero, sin, zero, sin], axis=1))


def kernel(x_prompt, x_sample, cache_fox_k, cache_fox_v, cache_fox_logf, cache_diff_k, cache_diff_v, page_table, meta_tokens, g_attn, w_in, b_f, lam_q1, lam_k1, lam_q2, lam_k2, g_fox, g_diff, w_o, g_ffn, w_pq, sub_k1, sub_k2, expert_u, expert_v, g_final):
    assert w_in.shape[0] == 1, "single-layer trunk"
    nb, seq, _ = x_prompt.shape
    db = x_sample.shape[0]
    n_pool = cache_fox_k.shape[1]
    n_pages = page_table.shape[1]
    lp_true = seq + N_META
    lp = -(-lp_true // Q_TILE) * Q_TILE

    w = w_in[0]
    c = [0, 512, 1024, 1536, 1544, 2056, 2568, 3080]
    w_re = jnp.concatenate([w[:, c[0]:c[3]], w[:, c[4]:c[7]], w[:, c[3]:c[4]],
                            jnp.zeros((D_MODEL, LANES - FOX_HEADS), F32)], axis=1).astype(BF16)
    bf_pad = jnp.concatenate([b_f[0], jnp.zeros((LANES - FOX_HEADS,), F32)])[None]
    g_attn2, g_ffn2, g_fox2, g_diff2 = g_attn[0][None], g_ffn[0][None], g_fox[0][None], g_diff[0][None]
    lams = (lam_q1[0][None], lam_k1[0][None], lam_q2[0][None], lam_k2[0][None])
    w_o16 = w_o[0].astype(BF16)
    w_pq16 = w_pq[0].astype(BF16)
    k1_16 = sub_k1[0].astype(BF16)
    k2_16 = sub_k2[0].astype(BF16)
    u_tab = _pack_table(expert_u[0])
    v_tab = _pack_table(expert_v[0])
    g_fin = g_final[None]

    xp = jnp.concatenate([jnp.broadcast_to(meta_tokens[None], (nb, N_META, D_MODEL)), x_prompt,
                          jnp.zeros((nb, lp - lp_true, D_MODEL), F32)], axis=1)
    cos, sa, sb = _rope_tables(jnp.arange(lp))
    (fq16, fk32, fk16, fv32, fv16, dq16, dk32, dk16, dv32, dv16, lf) = _inproj(
        xp, g_attn2, w_re, bf_pad, cos, sa, sb, lp_true, lp // 8)

    lf_t = jnp.pad(jnp.swapaxes(lf, 1, 2), ((0, 0), (0, 0), (0, lp - lp_true)))
    cum_t = _cumsum_lanes(lf_t.reshape(nb * FOX_HEADS, lp)).reshape(nb, FOX_HEADS, lp)
    cum = jnp.swapaxes(cum_t, 1, 2)
    mixed = _attention(fq16, dq16, fk16, fv16, dk16, dv16, cum, cum_t, lams, g_fox2, g_diff2)

    x2, h2, idx_t, gate_t = _mix(xp, mixed, w_o16, g_ffn2, w_pq16, k1_16, k2_16, Q_TILE)
    tb = 48
    idx = jnp.swapaxes(idx_t, 1, 2)[:, :lp_true]
    gate = jnp.swapaxes(gate_t, 1, 2)[:, :lp_true]
    h2_tiles = h2.reshape(nb, lp * TOKEN_ROWS, LANES)
    wts = _peer_u(idx, h2_tiles, gate, u_tab, tb)
    peer = _peer_v(idx, wts, v_tab, tb).reshape(nb, lp_true, D_MODEL)
    y_all = _final(x2, peer, g_fin, lp_true // 6)
    y_prompt = y_all[:, N_META:]

    cos_s, sa_s, sb_s = _rope_tables(jnp.full((db,), n_pages * PAGE))
    xs = x_sample.reshape(1, db, D_MODEL)
    (sfq16, sfk32, sfk16, sfv32, sfv16, sdq16, sdk32, sdk16, sdv32, sdv16, slf) = _inproj(
        xs, g_attn2, w_re, bf_pad, cos_s, sa_s, sb_s, db, db)
    tok = lambda a: a.reshape(db, 1, a.shape[-1])
    cache_lf_t = jnp.swapaxes(cache_fox_logf[0], 1, 2)
    mixed_s = _decode(page_table, tok(sfq16), tok(sdq16), tok(sfk16), tok(sfv16), tok(sdk16),
                      tok(sdv16), slf.reshape(db, FOX_HEADS, 1),
                      cache_fox_k[0].reshape(n_pool, PAGE, GROUP_W),
                      cache_fox_v[0].reshape(n_pool, PAGE, GROUP_W), cache_lf_t,
                      cache_diff_k[0].reshape(n_pool, PAGE, GROUP_W),
                      cache_diff_v[0].reshape(n_pool, PAGE, GROUP_W), lams, g_fox2, g_diff2)
    sx2, sh2, sidx_t, sgate_t = _mix(xs, mixed_s.reshape(1, db, MIX_W), w_o16, g_ffn2, w_pq16,
                                     k1_16, k2_16, db)
    sidx = jnp.swapaxes(sidx_t, 1, 2)
    sgate = jnp.swapaxes(sgate_t, 1, 2)
    swts = _peer_u(sidx, sh2.reshape(1, db * TOKEN_ROWS, LANES), sgate, u_tab, db)
    speer = _peer_v(sidx, swts, v_tab, db).reshape(1, db, D_MODEL)
    y_sample = _final(sx2, speer, g_fin, db).reshape(db, 1, D_MODEL)

    heads = lambda a, h: a.reshape((1, a.shape[0], a.shape[1], h, a.shape[2] // h))
    sheads = lambda a, h: a.reshape((1, db, 1, h, a.shape[-1] // h))
    return (y_prompt, y_sample,
            heads(fk32, 8), heads(fv32, 8), lf[None], heads(dk32, 4), heads(dv32, 4),
            sheads(sfk32, 8), sheads(sfv32, 8), slf.reshape(1, db, 1, FOX_HEADS),
            sheads(sdk32, 4), sheads(sdv32, 4))
```

```python
import functools

import jax
import jax.numpy as jnp
from jax import lax
from jax.experimental import pallas as pl
from jax.experimental.pallas import tpu as pltpu

F32 = jnp.float32
BF16 = jnp.bfloat16
I32 = jnp.int32

D_MODEL = 1024
N_META = 16
PAGE = 128
FOX_HEADS = 8
HEAD_DIM = 64
GROUP_W = 512
MIX_W = 1024
ROPE_THETA = 10000.0
RMS_EPS = 1e-6
QK_SCALE = 0.125
LAM_INIT = 0.2
PEER_HEADS = 8
PEER_KEYS = 128
PEER_TOPK = 16
PEER_PICKS = PEER_HEADS * PEER_TOPK
LANES = 128
SUBLANES = 8
Q_TILE = 128
DEC_PAGES = 4
N_CHAINS = 16
NEG_INF = float("-inf")
W_IN_COLS = 6 * GROUP_W + LANES

VMEM_LIMIT = 56 * 1024 * 1024


def _params(n_grid, vmem=VMEM_LIMIT):
    return pltpu.CompilerParams(dimension_semantics=("arbitrary",) * n_grid,
                                vmem_limit_bytes=vmem)


def _rms(x, g):
    return x * lax.rsqrt(jnp.mean(x * x, axis=-1, keepdims=True) + RMS_EPS) * g


def _dot_t(a, b):
    return lax.dot_general(a, b, (((1,), (1,)), ((), ())), preferred_element_type=F32)


def _dot(a, b):
    return jnp.dot(a, b, preferred_element_type=F32)


def _lambda(lq1, lk1, lq2, lk2):
    return (jnp.exp(jnp.sum(lq1[...] * lk1[...], axis=1, keepdims=True))
            - jnp.exp(jnp.sum(lq2[...] * lk2[...], axis=1, keepdims=True)) + LAM_INIT)


def _inproj_kernel(x_ref, g_ref, w_ref, bf_ref, cos_ref, sa_ref, sb_ref,
                   fq16, fk32, fv32, dq16, dk32, dv32, lf_ref, *attn_copies):
    x = x_ref[0]
    h16 = _rms(x, g_ref[...]).astype(BF16)

    def seg(k, width=GROUP_W):
        return _dot(h16, w_ref[:, GROUP_W * k:GROUP_W * k + width])

    cos = jnp.concatenate([cos_ref[...]] * 4, axis=1)
    sa = jnp.concatenate([sa_ref[...]] * 4, axis=1)
    sb = jnp.concatenate([sb_ref[...]] * 4, axis=1)

    def rope(v):
        return (v * cos + pltpu.roll(v, GROUP_W - HEAD_DIM // 2, axis=1) * sa
                + pltpu.roll(v, HEAD_DIM // 2, axis=1) * sb)

    fq16[0] = (seg(0) * QK_SCALE).astype(BF16)
    fk = seg(1)
    fk32[0] = fk
    fv = seg(2)
    fv32[0] = fv
    dq16[0] = (rope(seg(3)) * QK_SCALE).astype(BF16)
    dk = rope(seg(4))
    dk32[0] = dk
    dv = seg(5)
    dv32[0] = dv
    ff = seg(6, LANES) + bf_ref[...]
    lf_ref[0] = (jnp.minimum(ff, 0.0) - jnp.log1p(jnp.exp(-jnp.abs(ff))))[:, :FOX_HEADS]
    if attn_copies:
        fk16, fvt16, dk16, dvt16 = attn_copies
        fk16[0] = fk.astype(BF16)
        dk16[0] = dk.astype(BF16)
        fvt16[0] = fv.T.astype(BF16)
        dvt16[0] = dv.T.astype(BF16)


def _inproj(x, g_attn, w_re, bf_pad, cos, sa, sb, rows_out, tm, attn_copies):
    n, rows_in, _ = x.shape
    grid = (n, rows_in // tm)
    row_blk = lambda w: pl.BlockSpec((1, tm, w), lambda b, j: (b, j, 0))
    col_blk = pl.BlockSpec((1, GROUP_W, tm), lambda b, j: (b, 0, j))
    const = lambda shape: pl.BlockSpec(shape, lambda b, j: (0,) * len(shape))
    tab = pl.BlockSpec((tm, LANES), lambda b, j: (j, 0))
    o16 = jax.ShapeDtypeStruct((n, rows_in, GROUP_W), BF16)
    o16t = jax.ShapeDtypeStruct((n, GROUP_W, rows_in), BF16)
    o32 = jax.ShapeDtypeStruct((n, rows_out, GROUP_W), F32)
    out_specs = [row_blk(GROUP_W)] * 6 + [row_blk(FOX_HEADS)]
    out_shape = [o16, o32, o32, o16, o32, o32, jax.ShapeDtypeStruct((n, rows_out, FOX_HEADS), F32)]
    if attn_copies:
        out_specs += [row_blk(GROUP_W), col_blk, row_blk(GROUP_W), col_blk]
        out_shape += [o16, o16t, o16, o16t]
    return pl.pallas_call(
        _inproj_kernel,
        grid=grid,
        in_specs=[row_blk(D_MODEL), const((1, D_MODEL)), const((D_MODEL, W_IN_COLS)),
                  const((1, LANES)), tab, tab, tab],
        out_specs=out_specs,
        out_shape=out_shape,
        compiler_params=_params(2),
        name="inproj",
    )(x, g_attn, w_re, bf_pad, cos, sa, sb)


def _cumsum_kernel(x_ref, o_ref):
    rows, cols = x_ref.shape
    lane = lax.broadcasted_iota(I32, (rows, LANES), 1)
    carry = jnp.zeros((rows, 1), F32)
    for c in range(cols // LANES):
        x = x_ref[:, LANES * c:LANES * (c + 1)]
        s = 1
        while s < LANES:
            x = x + jnp.where(lane >= s, pltpu.roll(x, s, axis=1), 0.0)
            s *= 2
        x = x + carry
        o_ref[:, LANES * c:LANES * (c + 1)] = x
        carry = x[:, LANES - 1:LANES]


def _cumsum_lanes(x):
    return pl.pallas_call(
        _cumsum_kernel,
        out_shape=jax.ShapeDtypeStruct(x.shape, F32),
        name="cumsum",
    )(x)


def _attn_kernel(fq_ref, dq_ref, fk_ref, fvt_ref, dk_ref, dvt_ref, cum_ref, cumt_ref,
                 lq1, lk1, lq2, lk2, gfox_ref, gdiff_ref, o_ref, ckb, qh, s_sc, p_sc, *accs):
    t = pl.program_id(1)
    lp = fk_ref.shape[1]

    @pl.when(t == 0)
    def _():
        for h in range(FOX_HEADS):
            ckb[h] = jnp.broadcast_to(cum_ref[0, :, h:h + 1], (lp, LANES))

    key = lax.broadcasted_iota(I32, (Q_TILE, Q_TILE), 0)
    qry = lax.broadcasted_iota(I32, (Q_TILE, Q_TILE), 1)
    lane_lo = qry < HEAD_DIM
    qstart = pl.multiple_of(t * Q_TILE, Q_TILE)

    chains = []
    for kind, (q_ref, k_ref, vt_ref) in enumerate(((fq_ref, fk_ref, fvt_ref), (dq_ref, dk_ref, dvt_ref))):
        for g in range(4):
            q = q_ref[0, :, LANES * g:LANES * (g + 1)].astype(F32)
            for half in range(2):
                c = 8 * kind + 2 * g + half
                keep = lane_lo if half == 0 else jnp.logical_not(lane_lo)
                qh[c] = jnp.where(keep, q, 0.0).astype(BF16)
                v_rows = (LANES * g + HEAD_DIM * half, HEAD_DIM) if kind == 0 else (LANES * g, LANES)
                accs[c][...] = jnp.zeros(accs[c].shape, F32)
                chains.append((c, kind, g, 2 * g + half, k_ref, vt_ref, v_rows))

    chain_row = lax.broadcasted_iota(I32, (N_CHAINS, Q_TILE), 0)

    def body(j, carry):
        m_all, l_all = carry
        start = pl.multiple_of(j * Q_TILE, Q_TILE)
        visible = key <= qry + jnp.where(j < t, Q_TILE, 0)
        for c, kind, g, h, k_ref, vt_ref, (r0, nr) in chains:
            k = k_ref[0, pl.ds(start, Q_TILE), LANES * g:LANES * (g + 1)]
            s_sc[c] = _dot_t(k, qh[c])
        alphas = []
        for c, kind, g, h, k_ref, vt_ref, (r0, nr) in chains:
            s = s_sc[c]
            if kind == 0:
                cq = cumt_ref[0, h:h + 1, pl.ds(qstart, Q_TILE)]
                s = s + (cq - ckb[h, pl.ds(start, Q_TILE), :])
            s = jnp.where(visible, s, NEG_INF)
            m = m_all[c:c + 1]
            m_new = jnp.maximum(m, jnp.max(s, axis=0, keepdims=True))
            alpha = jnp.exp(m - m_new)
            p = jnp.exp(s - m_new)
            l_new = alpha * l_all[c:c + 1] + jnp.sum(p, axis=0, keepdims=True)
            m_all = jnp.where(chain_row == c, m_new, m_all)
            l_all = jnp.where(chain_row == c, l_new, l_all)
            p_sc[c] = p.astype(BF16)
            alphas.append(alpha)
        for c, kind, g, h, k_ref, vt_ref, (r0, nr) in chains:
            vt = vt_ref[0, r0:r0 + nr, pl.ds(start, Q_TILE)]
            accs[c][...] = alphas[c] * accs[c][...] + _dot(vt, p_sc[c])
        return m_all, l_all

    init = (jnp.full((N_CHAINS, Q_TILE), NEG_INF, F32), jnp.zeros((N_CHAINS, Q_TILE), F32))
    _, l_all = lax.fori_loop(0, t + 1, body, init)

    out = [accs[c][...] / l_all[c:c + 1] for c in range(N_CHAINS)]
    fox = [jnp.concatenate([out[2 * g], out[2 * g + 1]], axis=0).T for g in range(4)]
    o_f = _rms(jnp.concatenate(fox, axis=1), gfox_ref[...])
    lam = _lambda(lq1, lk1, lq2, lk2)
    diff = []
    for g in range(4):
        od = (out[8 + 2 * g] - lam * out[8 + 2 * g + 1]).T
        diff.append(_rms(od, gdiff_ref[...]) * (1.0 - LAM_INIT))
    o_ref[0] = jnp.concatenate([o_f] + diff, axis=1).astype(BF16)


def _attention(fq16, dq16, fk16, fvt16, dk16, dvt16, cum, cum_t, lams, g_fox, g_diff):
    n, lp, _ = fq16.shape
    grid = (n, lp // Q_TILE)
    tile = lambda w: pl.BlockSpec((1, Q_TILE, w), lambda b, t: (b, t, 0))
    full = lambda r, w: pl.BlockSpec((1, r, w), lambda b, t: (b, 0, 0))
    const = lambda shape: pl.BlockSpec(shape, lambda b, t: (0,) * len(shape))
    return pl.pallas_call(
        _attn_kernel,
        grid=grid,
        in_specs=[tile(GROUP_W), tile(GROUP_W), full(lp, GROUP_W), full(GROUP_W, lp),
                  full(lp, GROUP_W), full(GROUP_W, lp), full(lp, FOX_HEADS), full(FOX_HEADS, lp)]
                 + [const((1, HEAD_DIM))] * 4 + [const((1, GROUP_W)), const((1, LANES))],
        out_specs=tile(MIX_W),
        out_shape=jax.ShapeDtypeStruct((n, lp, MIX_W), BF16),
        scratch_shapes=[pltpu.VMEM((FOX_HEADS, lp, LANES), F32),
                        pltpu.VMEM((N_CHAINS, Q_TILE, LANES), BF16),
                        pltpu.VMEM((N_CHAINS, Q_TILE, Q_TILE), F32),
                        pltpu.VMEM((N_CHAINS, Q_TILE, Q_TILE), BF16)]
                       + [pltpu.VMEM((HEAD_DIM, Q_TILE), F32)] * 8
                       + [pltpu.VMEM((LANES, Q_TILE), F32)] * 8,
        compiler_params=_params(2),
        name="attention",
    )(fq16, dq16, fk16, fvt16, dk16, dvt16, cum, cum_t, *lams, g_fox, g_diff)


def _col_bcast(row):
    return jnp.concatenate(
        [jnp.broadcast_to(row[:, LANES * g:LANES * (g + 1)], (LANES, LANES)).T for g in range(4)],
        axis=0)


def _row_from_col(col):
    return jnp.concatenate(
        [jnp.broadcast_to(col[LANES * g:LANES * (g + 1)], (LANES, LANES)).T[0:1] for g in range(4)],
        axis=1)


def _decode_kernel(pt_ref, fq_ref, dq_ref, fkn_ref, fvn_ref, dkn_ref, dvn_ref, lfn_ref, *refs):
    pp = DEC_PAGES
    kt, vt, lft, dk, dv = (refs[i * pp:(i + 1) * pp] for i in range(5))
    (lq1, lk1, lq2, lk2, gfox_ref, gdiff_ref, o_ref,
     qb, mf, lf, af, md, ld, ad, carry) = refs[5 * pp:]
    p = pl.program_id(1)
    heads = FOX_HEADS

    def head_rows(row):
        return jnp.concatenate([row[:, LANES * (j // 2):LANES * (j // 2 + 1)] for j in range(heads)],
                               axis=0)

    sub = lax.broadcasted_iota(I32, (heads, LANES), 0)
    lane = lax.broadcasted_iota(I32, (heads, LANES), 1)
    qd = jnp.where((lane // HEAD_DIM) == (sub % 2), head_rows(dq_ref[0].astype(F32)), 0.0)

    @pl.when(p == 0)
    def _():
        fq = fq_ref[0].astype(F32)
        qb[...] = _col_bcast(fq)
        hrow = lax.broadcasted_iota(I32, (heads, GROUP_W), 0)
        hlane = lax.broadcasted_iota(I32, (heads, GROUP_W), 1)
        own = (hlane // HEAD_DIM) == hrow
        mf[...] = jnp.sum(jnp.where(own, fq * fkn_ref[0], 0.0), axis=1, keepdims=True)
        lf[...] = jnp.ones_like(lf)
        lane0 = lax.broadcasted_iota(I32, (GROUP_W, LANES), 1) == 0
        af[...] = jnp.where(lane0, _col_bcast(fvn_ref[0]), 0.0)
        md[...] = jnp.sum(qd * head_rows(dkn_ref[0]), axis=1, keepdims=True)
        ld[...] = jnp.ones_like(ld)
        ad[...] = head_rows(dvn_ref[0])
        carry[...] = lfn_ref[0]

    c = carry[...]
    scores = []
    for i in range(pp):
        x = lft[i][0]
        incl = x
        step = 1
        while step < PAGE:
            incl = incl + jnp.where(lane < PAGE - step, pltpu.roll(incl, PAGE - step, axis=1), 0.0)
            step *= 2
        bias = (incl - x) + c
        c = c + incl[:, 0:1]
        rows = [jnp.sum(kt[i][0, h] * qb[HEAD_DIM * h:HEAD_DIM * (h + 1), :], axis=0, keepdims=True)
                for h in range(heads)]
        scores.append(jnp.concatenate(rows, axis=0) + bias)
    carry[...] = c
    m_old = mf[...]
    m_new = m_old
    for s in scores:
        m_new = jnp.maximum(m_new, jnp.max(s, axis=1, keepdims=True))
    alpha = jnp.exp(m_old - m_new)
    probs = [jnp.exp(s - m_new) for s in scores]
    l_new = alpha * lf[...]
    for pr in probs:
        l_new = l_new + jnp.sum(pr, axis=1, keepdims=True)
    lf[...] = l_new
    mf[...] = m_new
    for h in range(heads):
        rows_h = slice(HEAD_DIM * h, HEAD_DIM * (h + 1))
        acc = af[rows_h, :] * jnp.broadcast_to(alpha[h:h + 1, :], (HEAD_DIM, LANES))
        for i in range(pp):
            acc = acc + jnp.broadcast_to(probs[i][h:h + 1, :], (HEAD_DIM, LANES)) * vt[i][0, h]
        af[rows_h, :] = acc

    qd16 = qd.astype(BF16)
    drow = lax.broadcasted_iota(I32, (heads, PAGE * 4), 0)
    dcol = lax.broadcasted_iota(I32, (heads, PAGE * 4), 1)
    visible = (dcol % 4) == (drow // 2)
    scores = [jnp.where(visible, _dot_t(qd16, dk[i][0].astype(BF16)), NEG_INF) for i in range(pp)]
    m_old = md[...]
    m_new = m_old
    for s in scores:
        m_new = jnp.maximum(m_new, jnp.max(s, axis=1, keepdims=True))
    alpha = jnp.exp(m_old - m_new)
    l_new = alpha * ld[...]
    acc = alpha * ad[...]
    for i in range(pp):
        pr = jnp.exp(scores[i] - m_new)
        l_new = l_new + jnp.sum(pr, axis=1, keepdims=True)
        acc = acc + _dot(pr.astype(BF16), dv[i][0].astype(BF16))
    ld[...] = l_new
    md[...] = m_new
    ad[...] = acc

    @pl.when(p == pl.num_programs(1) - 1)
    def _():
        o_col = jnp.sum(af[...], axis=1, keepdims=True)
        l_col = jnp.concatenate([jnp.broadcast_to(lf[h:h + 1, :], (HEAD_DIM, 1)) for h in range(heads)],
                                axis=0)
        o_f = _rms(_row_from_col(o_col / l_col), gfox_ref[...])
        lam = _lambda(lq1, lk1, lq2, lk2)
        od = ad[...] / ld[...]
        parts = [_rms(od[2 * h:2 * h + 1] - lam * od[2 * h + 1:2 * h + 2], gdiff_ref[...])
                 * (1.0 - LAM_INIT) for h in range(4)]
        o_ref[0] = jnp.concatenate([o_f] + parts, axis=1).astype(BF16)


def _decode(page_table, fq16, dq16, fk_new, fv_new, dk_new, dv_new, lf_col,
            cache_kt, cache_vt, cache_lf_t, cache_dk, cache_dv, lams, g_fox, g_diff):
    nb, n_pages = page_table.shape
    pp = DEC_PAGES
    tok = lambda w: pl.BlockSpec((1, 1, w), lambda b, p, pt: (b, 0, 0))
    const = lambda shape: pl.BlockSpec(shape, lambda b, p, pt: (0,) * len(shape))

    def pages(block):
        zeros = (0,) * (len(block) - 1)
        return [pl.BlockSpec(block, lambda b, p, pt, i=i: (pt[b, n_pages - 1 - (p * pp + i)],) + zeros)
                for i in range(pp)]

    grid_spec = pltpu.PrefetchScalarGridSpec(
        num_scalar_prefetch=1,
        grid=(nb, n_pages // pp),
        in_specs=[tok(GROUP_W)] * 6 + [pl.BlockSpec((1, FOX_HEADS, 1), lambda b, p, pt: (b, 0, 0))]
                 + pages((1, FOX_HEADS, HEAD_DIM, PAGE)) + pages((1, FOX_HEADS, HEAD_DIM, PAGE))
                 + pages((1, FOX_HEADS, PAGE)) + pages((1, PAGE * 4, LANES)) + pages((1, PAGE * 4, LANES))
                 + [const((1, HEAD_DIM))] * 4 + [const((1, GROUP_W)), const((1, LANES))],
        out_specs=pl.BlockSpec((1, 1, MIX_W), lambda b, p, pt: (b, 0, 0)),
        scratch_shapes=[pltpu.VMEM((GROUP_W, LANES), F32),
                        pltpu.VMEM((FOX_HEADS, 1), F32), pltpu.VMEM((FOX_HEADS, 1), F32),
                        pltpu.VMEM((GROUP_W, LANES), F32),
                        pltpu.VMEM((FOX_HEADS, 1), F32), pltpu.VMEM((FOX_HEADS, 1), F32),
                        pltpu.VMEM((FOX_HEADS, LANES), F32),
                        pltpu.VMEM((FOX_HEADS, 1), F32)],
    )
    return pl.pallas_call(
        _decode_kernel,
        grid_spec=grid_spec,
        out_shape=jax.ShapeDtypeStruct((nb, 1, MIX_W), BF16),
        compiler_params=_params(2),
        name="decode",
    )(page_table, fq16, dq16, fk_new, fv_new, dk_new, dv_new, lf_col,
      *([cache_kt] * pp), *([cache_vt] * pp), *([cache_lf_t] * pp),
      *([cache_dk] * pp), *([cache_dv] * pp), *lams, g_fox, g_diff)


def _topk_rows(s, order, payload=None):
    big = jnp.int32(2 ** 30)
    vals, picks = [], []
    for _ in range(PEER_TOPK):
        m = jnp.max(s, axis=0, keepdims=True)
        r = jnp.min(jnp.where(s == m, order, big), axis=0, keepdims=True)
        hit = order == r
        vals.append(m)
        if payload is None:
            picks.append(r)
        else:
            picks.append(jnp.sum(jnp.where(hit, payload, 0), axis=0, keepdims=True))
        s = jnp.where(hit, NEG_INF, s)
    return jnp.concatenate(vals, axis=0), jnp.concatenate(picks, axis=0)


def _mix_kernel(x_ref, mixed_ref, wo_ref, g_ref, wpq_ref, k1_ref, k2_ref,
                x2_ref, h2_ref, idx_ref, gate_ref):
    tm = x_ref.shape[1]
    x2 = x_ref[0] + _dot(mixed_ref[0], wo_ref[...])
    x2_ref[0] = x2
    h16 = _rms(x2, g_ref[...]).astype(BF16)
    h2_ref[0] = h16.astype(F32)
    key_rank = lax.broadcasted_iota(I32, (PEER_KEYS, tm), 0)
    sub_rank = lax.broadcasted_iota(I32, (SUBLANES, tm), 0)
    cand_rank = jnp.concatenate(
        [lax.broadcasted_iota(I32, (PEER_TOPK, tm), 0)]
        + [sub_rank + PEER_TOPK * a for a in range(1, PEER_TOPK)], axis=0)
    for hd in range(PEER_HEADS):
        q = _dot(h16, wpq_ref[:, 2 * LANES * hd:2 * LANES * (hd + 1)])
        s1 = _dot_t(k1_ref[...], q[:, :LANES].astype(BF16))
        s2 = _dot_t(k2_ref[...], q[:, LANES:].astype(BF16))
        v1, i1 = _topk_rows(s1, key_rank)
        v2, i2 = _topk_rows(s2, key_rank)
        cand = jnp.concatenate(
            [v1[0:1] + v2] + [v1[a:a + 1] + v2[:SUBLANES] for a in range(1, PEER_TOPK)], axis=0)
        cidx = jnp.concatenate(
            [i1[0:1] * PEER_KEYS + i2]
            + [i1[a:a + 1] * PEER_KEYS + i2[:SUBLANES] for a in range(1, PEER_TOPK)], axis=0)
        sc, idx = _topk_rows(cand, cand_rank, cidx)
        e = jnp.exp(sc - sc[0:1])
        gate = e / jnp.sum(e, axis=0, keepdims=True)
        idx_ref[0, PEER_TOPK * hd:PEER_TOPK * (hd + 1), :] = idx
        gate_ref[0, PEER_TOPK * hd:PEER_TOPK * (hd + 1), :] = gate


def _mix(x, mixed, w_o16, g_ffn, w_pq16, k1_16, k2_16, tm):
    n, rows, _ = x.shape
    grid = (n, rows // tm)
    row_blk = lambda w: pl.BlockSpec((1, tm, w), lambda b, j: (b, j, 0))
    col_blk = pl.BlockSpec((1, PEER_PICKS, tm), lambda b, j: (b, 0, j))
    const = lambda shape: pl.BlockSpec(shape, lambda b, j: (0,) * len(shape))
    return pl.pallas_call(
        _mix_kernel,
        grid=grid,
        in_specs=[row_blk(D_MODEL), row_blk(MIX_W), const((MIX_W, D_MODEL)), const((1, D_MODEL)),
                  const((D_MODEL, 2 * LANES * PEER_HEADS)), const((PEER_KEYS, LANES)),
                  const((PEER_KEYS, LANES))],
        out_specs=[row_blk(D_MODEL), row_blk(D_MODEL), col_blk, col_blk],
        out_shape=[jax.ShapeDtypeStruct((n, rows, D_MODEL), F32),
                   jax.ShapeDtypeStruct((n, rows, D_MODEL), F32),
                   jax.ShapeDtypeStruct((n, PEER_PICKS, rows), I32),
                   jax.ShapeDtypeStruct((n, PEER_PICKS, rows), F32)],
        compiler_params=_params(2),
        name="mix",
    )(x, mixed, w_o16, g_ffn, w_pq16, k1_16, k2_16)


HALF_ROWS = 4
TOKEN_ROWS = 2 * HALF_ROWS
HI_MASK = -65536
TOKEN_UNROLL = 8


def _unrolled_loop(n, body):
    def chunk(c, _):
        for u in range(TOKEN_UNROLL):
            body(c * TOKEN_UNROLL + u, 0)
        return 0

    lax.fori_loop(0, n // TOKEN_UNROLL, chunk, 0)


def _red_stride(tb):
    return tb * PEER_PICKS + 8


def _unpack(word):
    hi = pltpu.bitcast(word & HI_MASK, F32)
    lo = pltpu.bitcast(word << 16, F32)
    return hi, lo


def _expert_row(tab_ref, e):
    return tab_ref[pl.ds(pl.multiple_of(e * HALF_ROWS, HALF_ROWS), HALF_ROWS), :]


def _peer_u_kernel(idx_ref, x_ref, gate_ref, tab_ref, w_ref, red, act):
    tb = gate_ref.shape[0]
    stride = _red_stride(tb)

    def gather(t, _):
        xt = x_ref[pl.ds(pl.multiple_of(t * TOKEN_ROWS, TOKEN_ROWS), TOKEN_ROWS), :]
        xa = xt[:HALF_ROWS]
        xb = xt[HALF_ROWS:]
        picks = idx_ref.at[t]
        base = pl.multiple_of(t * PEER_PICKS, PEER_PICKS)
        window = red.at[pl.ds(base, (HALF_ROWS - 1) * stride + PEER_PICKS)]
        for j in range(PEER_PICKS):
            hi, lo = _unpack(_expert_row(tab_ref, picks[j]))
            window[pl.ds(j, HALF_ROWS, stride=stride), :] = hi * xa + lo * xb
        return 0

    def reduce(t, _):
        base = pl.multiple_of(t * PEER_PICKS, PEER_PICKS)
        blk = lambda s: red[pl.ds(base + s * stride, PEER_PICKS), :]
        part = (blk(0) + blk(1)) + (blk(2) + blk(3))
        act[pl.ds(t, 1), :] = jnp.sum(part.T, axis=0, keepdims=True)
        return 0

    lax.fori_loop(0, tb, gather, 0)
    _unrolled_loop(tb, reduce)
    a = act[...]
    w = gate_ref[...] * (0.5 * a * (1.0 + lax.erf(a * (2.0 ** -0.5))))
    w_ref[...] = w.astype(BF16).astype(F32)


def _peer_v_kernel(idx_ref, w_ref, tab_ref, o_ref, wb):
    tb = idx_ref.shape[0]
    n_acc = 2

    def spread(t, _):
        wb[pl.ds(pl.multiple_of(t * PEER_PICKS, PEER_PICKS), PEER_PICKS), :] = (
            jnp.broadcast_to(w_ref[pl.ds(t, 1), :], (PEER_PICKS, LANES)).T)
        return 0

    _unrolled_loop(tb, spread)

    def token(t, _):
        acc_a = [jnp.zeros((HALF_ROWS, LANES), F32) for _ in range(n_acc)]
        acc_b = [jnp.zeros((HALF_ROWS, LANES), F32) for _ in range(n_acc)]
        picks = idx_ref.at[t]
        wt = wb.at[pl.ds(pl.multiple_of(t * PEER_PICKS, PEER_PICKS), PEER_PICKS)]
        for j in range(PEER_PICKS):
            hi, lo = _unpack(_expert_row(tab_ref, picks[j]))
            w = jnp.broadcast_to(wt[j:j + 1, :], (HALF_ROWS, LANES))
            acc_a[j % n_acc] = acc_a[j % n_acc] + w * hi
            acc_b[j % n_acc] = acc_b[j % n_acc] + w * lo
        a = functools.reduce(lambda u, v: u + v, acc_a)
        b = functools.reduce(lambda u, v: u + v, acc_b)
        o_ref[pl.ds(pl.multiple_of(t * TOKEN_ROWS, TOKEN_ROWS), TOKEN_ROWS), :] = (
            jnp.concatenate([a, b], axis=0))
        return 0

    lax.fori_loop(0, tb, token, 0)


def _resident(shape):
    return pl.BlockSpec(shape, lambda *_: (0,) * len(shape), pipeline_mode=pl.Buffered(1))


def _peer_u(idx, h2_tiles, gate, table, tb):
    n, rows, _ = idx.shape
    grid = (n, rows // tb)
    return pl.pallas_call(
        _peer_u_kernel,
        grid=grid,
        in_specs=[pl.BlockSpec((None, tb, PEER_PICKS), lambda b, j: (b, j, 0),
                               memory_space=pltpu.SMEM),
                  pl.BlockSpec((None, tb * TOKEN_ROWS, LANES), lambda b, j: (b, j, 0)),
                  pl.BlockSpec((None, tb, PEER_PICKS), lambda b, j: (b, j, 0)),
                  _resident(table.shape)],
        out_specs=pl.BlockSpec((None, tb, PEER_PICKS), lambda b, j: (b, j, 0)),
        out_shape=jax.ShapeDtypeStruct((n, rows, PEER_PICKS), F32),
        scratch_shapes=[pltpu.VMEM((HALF_ROWS * _red_stride(tb), LANES), F32),
                        pltpu.VMEM((tb, PEER_PICKS), F32)],
        compiler_params=_params(2),
        name="peer_u",
    )(idx, h2_tiles, gate, table)


def _peer_v(idx, w, table, tb):
    n, rows, _ = idx.shape
    grid = (n, rows // tb)
    return pl.pallas_call(
        _peer_v_kernel,
        grid=grid,
        in_specs=[pl.BlockSpec((None, tb, PEER_PICKS), lambda b, j: (b, j, 0),
                               memory_space=pltpu.SMEM),
                  pl.BlockSpec((None, tb, PEER_PICKS), lambda b, j: (b, j, 0)),
                  _resident(table.shape)],
        out_specs=pl.BlockSpec((None, tb * TOKEN_ROWS, LANES), lambda b, j: (b, j, 0)),
        out_shape=jax.ShapeDtypeStruct((n, rows * TOKEN_ROWS, LANES), F32),
        scratch_shapes=[pltpu.VMEM((tb * PEER_PICKS, LANES), F32)],
        compiler_params=_params(2),
        name="peer_v",
    )(idx, w, table)


def _pack_table(t):
    t16 = t.astype(BF16)
    hi = lax.bitcast_convert_type(t16[:, :GROUP_W], jnp.uint16).astype(jnp.uint32)
    lo = lax.bitcast_convert_type(t16[:, GROUP_W:], jnp.uint16).astype(jnp.uint32)
    packed = lax.bitcast_convert_type((hi << 16) | lo, I32)
    return packed.reshape(t.shape[0] * HALF_ROWS, LANES)


def _final_kernel(x_ref, p_ref, g_ref, y_ref):
    y_ref[0] = _rms(x_ref[0] + p_ref[0], g_ref[...])


def _final(x2, peer, g_final, tm):
    n, rows, _ = peer.shape
    grid = (n, rows // tm)
    blk = pl.BlockSpec((1, tm, D_MODEL), lambda b, j: (b, j, 0))
    return pl.pallas_call(
        _final_kernel,
        grid=grid,
        in_specs=[blk, blk, pl.BlockSpec((1, D_MODEL), lambda b, j: (0, 0))],
        out_specs=blk,
        out_shape=jax.ShapeDtypeStruct((n, rows, D_MODEL), F32),
        compiler_params=_params(2),
        name="final",
    )(x2, peer, g_final)


def _rope_tables(pos):
    half = HEAD_DIM // 2
    inv = ROPE_THETA ** (-jnp.arange(half, dtype=F32) / half)
    ang = pos.astype(F32)[:, None] * inv[None, :]
    cos, sin = jnp.cos(ang), jnp.sin(ang)
    zero = jnp.zeros_like(sin)
    return (jnp.concatenate([cos, cos, cos, cos], axis=1),
            jnp.concatenate([-sin, zero, -sin, zero], axis=1),
            jnp.concatenate([zero, sin, zero, sin], axis=1))


def kernel(x_prompt, x_sample, cache_fox_k, cache_fox_v, cache_fox_logf, cache_diff_k, cache_diff_v, page_table, meta_tokens, g_attn, w_in, b_f, lam_q1, lam_k1, lam_q2, lam_k2, g_fox, g_diff, w_o, g_ffn, w_pq, sub_k1, sub_k2, expert_u, expert_v, g_final):
    assert w_in.shape[0] == 1, "single-layer trunk"
    nb, seq, _ = x_prompt.shape
    db = x_sample.shape[0]
    n_pool = cache_fox_k.shape[1]
    n_pages = page_table.shape[1]
    lp_true = seq + N_META
    lp = -(-lp_true // Q_TILE) * Q_TILE
    peer_tb = 48
    final_tm = lp_true // 6

    w = w_in[0]
    c = [0, 512, 1024, 1536, 1544, 2056, 2568, 3080]
    w_re = jnp.concatenate([w[:, c[0]:c[3]], w[:, c[4]:c[7]], w[:, c[3]:c[4]],
                            jnp.zeros((D_MODEL, LANES - FOX_HEADS), F32)], axis=1).astype(BF16)
    bf_pad = jnp.concatenate([b_f[0], jnp.zeros((LANES - FOX_HEADS,), F32)])[None]
    g_attn2, g_ffn2, g_fox2, g_diff2 = g_attn[0][None], g_ffn[0][None], g_fox[0][None], g_diff[0][None]
    lams = (lam_q1[0][None], lam_k1[0][None], lam_q2[0][None], lam_k2[0][None])
    w_o16 = w_o[0].astype(BF16)
    w_pq16 = w_pq[0].astype(BF16)
    k1_16 = sub_k1[0].astype(BF16)
    k2_16 = sub_k2[0].astype(BF16)
    u_tab = _pack_table(expert_u[0])
    v_tab = _pack_table(expert_v[0])
    g_fin = g_final[None]

    xp = jnp.concatenate([jnp.broadcast_to(meta_tokens[None], (nb, N_META, D_MODEL)), x_prompt,
                          jnp.zeros((nb, lp - lp_true, D_MODEL), F32)], axis=1)
    cos, sa, sb = _rope_tables(jnp.arange(lp))
    (fq16, fk32, fv32, dq16, dk32, dv32, lf, fk16, fvt16, dk16, dvt16) = _inproj(
        xp, g_attn2, w_re, bf_pad, cos, sa, sb, lp_true, Q_TILE, True)

    lf_t = jnp.pad(jnp.swapaxes(lf, 1, 2), ((0, 0), (0, 0), (0, lp - lp_true)))
    cum_t = _cumsum_lanes(lf_t.reshape(nb * FOX_HEADS, lp)).reshape(nb, FOX_HEADS, lp)
    cum = jnp.swapaxes(cum_t, 1, 2)
    mixed = _attention(fq16, dq16, fk16, fvt16, dk16, dvt16, cum, cum_t, lams, g_fox2, g_diff2)

    x2, h2, idx_t, gate_t = _mix(xp, mixed, w_o16, g_ffn2, w_pq16, k1_16, k2_16, Q_TILE)
    idx = jnp.swapaxes(idx_t, 1, 2)[:, :lp_true]
    gate = jnp.swapaxes(gate_t, 1, 2)[:, :lp_true]
    h2_tiles = h2.reshape(nb, lp * TOKEN_ROWS, LANES)
    wts = _peer_u(idx, h2_tiles, gate, u_tab, peer_tb)
    peer = _peer_v(idx, wts, v_tab, peer_tb).reshape(nb, lp_true, D_MODEL)
    y_all = _final(x2, peer, g_fin, final_tm)
    y_prompt = y_all[:, N_META:]

    cos_s, sa_s, sb_s = _rope_tables(jnp.full((db,), n_pages * PAGE))
    xs = x_sample.reshape(1, db, D_MODEL)
    (sfq16, sfk32, sfv32, sdq16, sdk32, sdv32, slf) = _inproj(
        xs, g_attn2, w_re, bf_pad, cos_s, sa_s, sb_s, db, db, False)
    tok = lambda a: a.reshape(db, 1, a.shape[-1])
    cache_kt = jnp.transpose(cache_fox_k[0], (0, 2, 3, 1))
    cache_vt = jnp.transpose(cache_fox_v[0], (0, 2, 3, 1))
    cache_lf_t = jnp.swapaxes(cache_fox_logf[0], 1, 2)
    cache_dk = cache_diff_k[0].reshape(n_pool, PAGE * 4, LANES)
    cache_dv = cache_diff_v[0].reshape(n_pool, PAGE * 4, LANES)
    mixed_s = _decode(page_table, tok(sfq16), tok(sdq16), tok(sfk32), tok(sfv32), tok(sdk32),
                      tok(sdv32), slf.reshape(db, FOX_HEADS, 1),
                      cache_kt, cache_vt, cache_lf_t, cache_dk, cache_dv, lams, g_fox2, g_diff2)
    sx2, sh2, sidx_t, sgate_t = _mix(xs, mixed_s.reshape(1, db, MIX_W), w_o16, g_ffn2, w_pq16,
                                     k1_16, k2_16, db)
    sidx = jnp.swapaxes(sidx_t, 1, 2)
    sgate = jnp.swapaxes(sgate_t, 1, 2)
    swts = _peer_u(sidx, sh2.reshape(1, db * TOKEN_ROWS, LANES), sgate, u_tab, db)
    speer = _peer_v(sidx, swts, v_tab, db).reshape(1, db, D_MODEL)
    y_sample = _final(sx2, speer, g_fin, db).reshape(db, 1, D_MODEL)

    heads = lambda a, h: a.reshape((1, a.shape[0], a.shape[1], h, a.shape[2] // h))
    sheads = lambda a, h: a.reshape((1, db, 1, h, a.shape[-1] // h))
    return (y_prompt, y_sample,
            heads(fk32, 8), heads(fv32, 8), lf[None], heads(dk32, 4), heads(dv32, 4),
            sheads(sfk32, 8), sheads(sfv32, 8), slf.reshape(1, db, 1, FOX_HEADS),
            sheads(sdk32, 4), sheads(sdv32, 4))
```

```python
import functools

import jax
import jax.numpy as jnp
from jax import lax
from jax.experimental import pallas as pl
from jax.experimental.pallas import tpu as pltpu

F32 = jnp.float32
BF16 = jnp.bfloat16
I32 = jnp.int32

D_MODEL = 1024
N_META = 16
PAGE = 128
FOX_HEADS = 8
HEAD_DIM = 64
GROUP_W = 512
MIX_W = 1024
ROPE_THETA = 10000.0
RMS_EPS = 1e-6
QK_SCALE = 0.125
LAM_INIT = 0.2
PEER_HEADS = 8
PEER_KEYS = 128
PEER_TOPK = 16
PEER_PICKS = PEER_HEADS * PEER_TOPK
LANES = 128
Q_TILE = 128
DEC_PAGES = 8
N_CHAINS = 16
HALF_ROWS = 4
NEG_INF = float("-inf")
W_IN_COLS = 6 * GROUP_W + LANES

VMEM_LIMIT = 56 * 1024 * 1024


def _params(n_grid, vmem=VMEM_LIMIT):
    return pltpu.CompilerParams(dimension_semantics=("arbitrary",) * n_grid,
                                vmem_limit_bytes=vmem)


def _rms(x, g):
    return x * lax.rsqrt(jnp.mean(x * x, axis=-1, keepdims=True) + RMS_EPS) * g


def _dot_t(a, b):
    return lax.dot_general(a, b, (((1,), (1,)), ((), ())), preferred_element_type=F32)


def _dot(a, b):
    return jnp.dot(a, b, preferred_element_type=F32)


def _lambda(lq1, lk1, lq2, lk2):
    return (jnp.exp(jnp.sum(lq1[...] * lk1[...], axis=1, keepdims=True))
            - jnp.exp(jnp.sum(lq2[...] * lk2[...], axis=1, keepdims=True)) + LAM_INIT)


def _inproj_kernel(x_ref, g_ref, w_ref, bf_ref, cos_ref, sa_ref, sb_ref,
                   fq16, fk32, fv32, dq16, dk32, dv32, lf_ref, *attn_copies):
    x = x_ref[0]
    h16 = _rms(x, g_ref[...]).astype(BF16)

    def seg(k, width=GROUP_W):
        return _dot(h16, w_ref[:, GROUP_W * k:GROUP_W * k + width])

    cos = jnp.concatenate([cos_ref[...]] * 4, axis=1)
    sa = jnp.concatenate([sa_ref[...]] * 4, axis=1)
    sb = jnp.concatenate([sb_ref[...]] * 4, axis=1)

    def rope(v):
        return (v * cos + pltpu.roll(v, GROUP_W - HEAD_DIM // 2, axis=1) * sa
                + pltpu.roll(v, HEAD_DIM // 2, axis=1) * sb)

    fq16[0] = (seg(0) * QK_SCALE).astype(BF16)
    fk = seg(1)
    fk32[0] = fk
    fv = seg(2)
    fv32[0] = fv
    dq16[0] = (rope(seg(3)) * QK_SCALE).astype(BF16)
    dk = rope(seg(4))
    dk32[0] = dk
    dv = seg(5)
    dv32[0] = dv
    ff = seg(6, LANES) + bf_ref[...]
    lf_ref[0] = (jnp.minimum(ff, 0.0) - jnp.log1p(jnp.exp(-jnp.abs(ff))))[:, :FOX_HEADS]
    if attn_copies:
        fk16, fvt16, dk16, dvt16 = attn_copies
        fk16[0] = fk.astype(BF16)
        dk16[0] = dk.astype(BF16)
        fvt16[0] = fv.T.astype(BF16)
        dvt16[0] = dv.T.astype(BF16)


def _inproj(x, g_attn, w_re, bf_pad, cos, sa, sb, rows_out, tm, attn_copies):
    n, rows_in, _ = x.shape
    grid = (n, rows_in // tm)
    row_blk = lambda w: pl.BlockSpec((1, tm, w), lambda b, j: (b, j, 0))
    col_blk = pl.BlockSpec((1, GROUP_W, tm), lambda b, j: (b, 0, j))
    const = lambda shape: pl.BlockSpec(shape, lambda b, j: (0,) * len(shape))
    tab = pl.BlockSpec((tm, LANES), lambda b, j: (j, 0))
    o16 = jax.ShapeDtypeStruct((n, rows_in, GROUP_W), BF16)
    o16t = jax.ShapeDtypeStruct((n, GROUP_W, rows_in), BF16)
    o32 = jax.ShapeDtypeStruct((n, rows_out, GROUP_W), F32)
    out_specs = [row_blk(GROUP_W)] * 6 + [row_blk(FOX_HEADS)]
    out_shape = [o16, o32, o32, o16, o32, o32, jax.ShapeDtypeStruct((n, rows_out, FOX_HEADS), F32)]
    if attn_copies:
        out_specs += [row_blk(GROUP_W), col_blk, row_blk(GROUP_W), col_blk]
        out_shape += [o16, o16t, o16, o16t]
    return pl.pallas_call(
        _inproj_kernel,
        grid=grid,
        in_specs=[row_blk(D_MODEL), const((1, D_MODEL)), const((D_MODEL, W_IN_COLS)),
                  const((1, LANES)), tab, tab, tab],
        out_specs=out_specs,
        out_shape=out_shape,
        compiler_params=_params(2),
        name="inproj",
    )(x, g_attn, w_re, bf_pad, cos, sa, sb)


def _cumsum_kernel(x_ref, o_ref):
    rows, cols = x_ref.shape
    lane = lax.broadcasted_iota(I32, (rows, LANES), 1)
    carry = jnp.zeros((rows, 1), F32)
    for c in range(cols // LANES):
        x = x_ref[:, LANES * c:LANES * (c + 1)]
        s = 1
        while s < LANES:
            x = x + jnp.where(lane >= s, pltpu.roll(x, s, axis=1), 0.0)
            s *= 2
        x = x + carry
        o_ref[:, LANES * c:LANES * (c + 1)] = x
        carry = x[:, LANES - 1:LANES]


def _cumsum_lanes(x):
    return pl.pallas_call(
        _cumsum_kernel,
        out_shape=jax.ShapeDtypeStruct(x.shape, F32),
        name="cumsum",
    )(x)


def _attn_kernel(fq_ref, dq_ref, fk_ref, fvt_ref, dk_ref, dvt_ref, cum_ref, cumt_ref,
                 lq1, lk1, lq2, lk2, gfox_ref, gdiff_ref, o_ref, ckb, qh, s_sc, p_sc, *accs):
    t = pl.program_id(1)
    lp = fk_ref.shape[1]

    @pl.when(t == 0)
    def _():
        for h in range(FOX_HEADS):
            ckb[h] = jnp.broadcast_to(cum_ref[0, :, h:h + 1], (lp, LANES))

    key = lax.broadcasted_iota(I32, (Q_TILE, Q_TILE), 0)
    qry = lax.broadcasted_iota(I32, (Q_TILE, Q_TILE), 1)
    lane_lo = qry < HEAD_DIM
    qstart = pl.multiple_of(t * Q_TILE, Q_TILE)

    chains = []
    for kind, (q_ref, k_ref, vt_ref) in enumerate(((fq_ref, fk_ref, fvt_ref), (dq_ref, dk_ref, dvt_ref))):
        for g in range(4):
            q = q_ref[0, :, LANES * g:LANES * (g + 1)].astype(F32)
            for half in range(2):
                c = 8 * kind + 2 * g + half
                keep = lane_lo if half == 0 else jnp.logical_not(lane_lo)
                qh[c] = jnp.where(keep, q, 0.0).astype(BF16)
                v_rows = (LANES * g + HEAD_DIM * half, HEAD_DIM) if kind == 0 else (LANES * g, LANES)
                accs[c][...] = jnp.zeros(accs[c].shape, F32)
                chains.append((c, kind, g, 2 * g + half, k_ref, vt_ref, v_rows))

    chain_row = lax.broadcasted_iota(I32, (N_CHAINS, Q_TILE), 0)

    def body(j, carry):
        m_all, l_all = carry
        start = pl.multiple_of(j * Q_TILE, Q_TILE)
        visible = key <= qry + jnp.where(j < t, Q_TILE, 0)
        for c, kind, g, h, k_ref, vt_ref, (r0, nr) in chains:
            k = k_ref[0, pl.ds(start, Q_TILE), LANES * g:LANES * (g + 1)]
            s_sc[c] = _dot_t(k, qh[c])
        alphas = []
        for c, kind, g, h, k_ref, vt_ref, (r0, nr) in chains:
            s = s_sc[c]
            if kind == 0:
                cq = cumt_ref[0, h:h + 1, pl.ds(qstart, Q_TILE)]
                s = s + (cq - ckb[h, pl.ds(start, Q_TILE), :])
            s = jnp.where(visible, s, NEG_INF)
            m = m_all[c:c + 1]
            m_new = jnp.maximum(m, jnp.max(s, axis=0, keepdims=True))
            alpha = jnp.exp(m - m_new)
            p = jnp.exp(s - m_new)
            l_new = alpha * l_all[c:c + 1] + jnp.sum(p, axis=0, keepdims=True)
            m_all = jnp.where(chain_row == c, m_new, m_all)
            l_all = jnp.where(chain_row == c, l_new, l_all)
            p_sc[c] = p.astype(BF16)
            alphas.append(alpha)
        for c, kind, g, h, k_ref, vt_ref, (r0, nr) in chains:
            vt = vt_ref[0, r0:r0 + nr, pl.ds(start, Q_TILE)]
            accs[c][...] = alphas[c] * accs[c][...] + _dot(vt, p_sc[c])
        return m_all, l_all

    init = (jnp.full((N_CHAINS, Q_TILE), NEG_INF, F32), jnp.zeros((N_CHAINS, Q_TILE), F32))
    _, l_all = lax.fori_loop(0, t + 1, body, init)

    out = [accs[c][...] / l_all[c:c + 1] for c in range(N_CHAINS)]
    fox = [jnp.concatenate([out[2 * g], out[2 * g + 1]], axis=0).T for g in range(4)]
    o_f = _rms(jnp.concatenate(fox, axis=1), gfox_ref[...])
    lam = _lambda(lq1, lk1, lq2, lk2)
    diff = []
    for g in range(4):
        od = (out[8 + 2 * g] - lam * out[8 + 2 * g + 1]).T
        diff.append(_rms(od, gdiff_ref[...]) * (1.0 - LAM_INIT))
    o_ref[0] = jnp.concatenate([o_f] + diff, axis=1).astype(BF16)


def _attention(fq16, dq16, fk16, fvt16, dk16, dvt16, cum, cum_t, lams, g_fox, g_diff):
    n, lp, _ = fq16.shape
    grid = (n, lp // Q_TILE)
    tile = lambda w: pl.BlockSpec((1, Q_TILE, w), lambda b, t: (b, t, 0))
    full = lambda r, w: pl.BlockSpec((1, r, w), lambda b, t: (b, 0, 0))
    const = lambda shape: pl.BlockSpec(shape, lambda b, t: (0,) * len(shape))
    return pl.pallas_call(
        _attn_kernel,
        grid=grid,
        in_specs=[tile(GROUP_W), tile(GROUP_W), full(lp, GROUP_W), full(GROUP_W, lp),
                  full(lp, GROUP_W), full(GROUP_W, lp), full(lp, FOX_HEADS), full(FOX_HEADS, lp)]
                 + [const((1, HEAD_DIM))] * 4 + [const((1, GROUP_W)), const((1, LANES))],
        out_specs=tile(MIX_W),
        out_shape=jax.ShapeDtypeStruct((n, lp, MIX_W), BF16),
        scratch_shapes=[pltpu.VMEM((FOX_HEADS, lp, LANES), F32),
                        pltpu.VMEM((N_CHAINS, Q_TILE, LANES), BF16),
                        pltpu.VMEM((N_CHAINS, Q_TILE, Q_TILE), F32),
                        pltpu.VMEM((N_CHAINS, Q_TILE, Q_TILE), BF16)]
                       + [pltpu.VMEM((HEAD_DIM, Q_TILE), F32)] * 8
                       + [pltpu.VMEM((LANES, Q_TILE), F32)] * 8,
        compiler_params=_params(2),
        name="attention",
    )(fq16, dq16, fk16, fvt16, dk16, dvt16, cum, cum_t, *lams, g_fox, g_diff)


def _col_bcast(row):
    return jnp.concatenate(
        [jnp.broadcast_to(row[:, LANES * g:LANES * (g + 1)], (LANES, LANES)).T for g in range(4)],
        axis=0)


def _row_from_col(col):
    return jnp.concatenate(
        [jnp.broadcast_to(col[LANES * g:LANES * (g + 1)], (LANES, LANES)).T[0:1] for g in range(4)],
        axis=1)


def _decode_kernel(pt_ref, fq_ref, dq_ref, fkn_ref, fvn_ref, dkn_ref, dvn_ref, lfn_ref, *refs):
    pp = DEC_PAGES
    kt, vt, lft, dk, dv = (refs[i * pp:(i + 1) * pp] for i in range(5))
    (lq1, lk1, lq2, lk2, gfox_ref, gdiff_ref, o_ref,
     qb, mf, lf, af, md, ld, ad, carry) = refs[5 * pp:]
    p = pl.program_id(1)
    heads = FOX_HEADS

    def head_rows(row):
        return jnp.concatenate([row[:, LANES * (j // 2):LANES * (j // 2 + 1)] for j in range(heads)],
                               axis=0)

    sub = lax.broadcasted_iota(I32, (heads, LANES), 0)
    lane = lax.broadcasted_iota(I32, (heads, LANES), 1)
    qd = jnp.where((lane // HEAD_DIM) == (sub % 2), head_rows(dq_ref[0].astype(F32)), 0.0)

    @pl.when(p == 0)
    def _():
        fq = fq_ref[0].astype(F32)
        qb[...] = _col_bcast(fq)
        hrow = lax.broadcasted_iota(I32, (heads, GROUP_W), 0)
        hlane = lax.broadcasted_iota(I32, (heads, GROUP_W), 1)
        own = (hlane // HEAD_DIM) == hrow
        mf[...] = jnp.sum(jnp.where(own, fq * fkn_ref[0], 0.0), axis=1, keepdims=True)
        lf[...] = jnp.ones_like(lf)
        lane0 = lax.broadcasted_iota(I32, (GROUP_W, LANES), 1) == 0
        af[...] = jnp.where(lane0, _col_bcast(fvn_ref[0]), 0.0)
        md[...] = jnp.sum(qd * head_rows(dkn_ref[0]), axis=1, keepdims=True)
        ld[...] = jnp.ones_like(ld)
        ad[...] = head_rows(dvn_ref[0])
        carry[...] = lfn_ref[0]

    c = carry[...]
    scores = []
    for i in range(pp):
        x = lft[i][0]
        incl = x
        step = 1
        while step < PAGE:
            incl = incl + jnp.where(lane < PAGE - step, pltpu.roll(incl, PAGE - step, axis=1), 0.0)
            step *= 2
        bias = (incl - x) + c
        c = c + incl[:, 0:1]
        rows = [jnp.sum(kt[i][0, h] * qb[HEAD_DIM * h:HEAD_DIM * (h + 1), :], axis=0, keepdims=True)
                for h in range(heads)]
        scores.append(jnp.concatenate(rows, axis=0) + bias)
    carry[...] = c
    m_old = mf[...]
    m_new = m_old
    for s in scores:
        m_new = jnp.maximum(m_new, jnp.max(s, axis=1, keepdims=True))
    alpha = jnp.exp(m_old - m_new)
    probs = [jnp.exp(s - m_new) for s in scores]
    l_new = alpha * lf[...]
    for pr in probs:
        l_new = l_new + jnp.sum(pr, axis=1, keepdims=True)
    lf[...] = l_new
    mf[...] = m_new
    for h in range(heads):
        rows_h = slice(HEAD_DIM * h, HEAD_DIM * (h + 1))
        acc = af[rows_h, :] * jnp.broadcast_to(alpha[h:h + 1, :], (HEAD_DIM, LANES))
        for i in range(pp):
            acc = acc + jnp.broadcast_to(probs[i][h:h + 1, :], (HEAD_DIM, LANES)) * vt[i][0, h]
        af[rows_h, :] = acc

    qd16 = qd.astype(BF16)
    drow = lax.broadcasted_iota(I32, (heads, PAGE * 4), 0)
    dcol = lax.broadcasted_iota(I32, (heads, PAGE * 4), 1)
    visible = (dcol % 4) == (drow // 2)
    scores = [jnp.where(visible, _dot_t(qd16, dk[i][0].astype(BF16)), NEG_INF) for i in range(pp)]
    m_old = md[...]
    m_new = m_old
    for s in scores:
        m_new = jnp.maximum(m_new, jnp.max(s, axis=1, keepdims=True))
    alpha = jnp.exp(m_old - m_new)
    l_new = alpha * ld[...]
    acc = alpha * ad[...]
    for i in range(pp):
        pr = jnp.exp(scores[i] - m_new)
        l_new = l_new + jnp.sum(pr, axis=1, keepdims=True)
        acc = acc + _dot(pr.astype(BF16), dv[i][0].astype(BF16))
    ld[...] = l_new
    md[...] = m_new
    ad[...] = acc

    @pl.when(p == pl.num_programs(1) - 1)
    def _():
        o_col = jnp.sum(af[...], axis=1, keepdims=True)
        l_col = jnp.concatenate([jnp.broadcast_to(lf[h:h + 1, :], (HEAD_DIM, 1)) for h in range(heads)],
                                axis=0)
        o_f = _rms(_row_from_col(o_col / l_col), gfox_ref[...])
        lam = _lambda(lq1, lk1, lq2, lk2)
        od = ad[...] / ld[...]
        parts = [_rms(od[2 * h:2 * h + 1] - lam * od[2 * h + 1:2 * h + 2], gdiff_ref[...])
                 * (1.0 - LAM_INIT) for h in range(4)]
        o_ref[0] = jnp.concatenate([o_f] + parts, axis=1).astype(BF16)


def _decode(page_table, fq16, dq16, fk_new, fv_new, dk_new, dv_new, lf_col,
            cache_kt, cache_vt, cache_lf_t, cache_dk, cache_dv, lams, g_fox, g_diff):
    nb, n_pages = page_table.shape
    pp = DEC_PAGES
    tok = lambda w: pl.BlockSpec((1, 1, w), lambda b, p, pt: (b, 0, 0))
    const = lambda shape: pl.BlockSpec(shape, lambda b, p, pt: (0,) * len(shape))

    def pages(block):
        zeros = (0,) * (len(block) - 1)
        return [pl.BlockSpec(block, lambda b, p, pt, i=i: (pt[b, n_pages - 1 - (p * pp + i)],) + zeros)
                for i in range(pp)]

    grid_spec = pltpu.PrefetchScalarGridSpec(
        num_scalar_prefetch=1,
        grid=(nb, n_pages // pp),
        in_specs=[tok(GROUP_W)] * 6 + [pl.BlockSpec((1, FOX_HEADS, 1), lambda b, p, pt: (b, 0, 0))]
                 + pages((1, FOX_HEADS, HEAD_DIM, PAGE)) + pages((1, FOX_HEADS, HEAD_DIM, PAGE))
                 + pages((1, FOX_HEADS, PAGE)) + pages((1, PAGE * 4, LANES)) + pages((1, PAGE * 4, LANES))
                 + [const((1, HEAD_DIM))] * 4 + [const((1, GROUP_W)), const((1, LANES))],
        out_specs=pl.BlockSpec((1, 1, MIX_W), lambda b, p, pt: (b, 0, 0)),
        scratch_shapes=[pltpu.VMEM((GROUP_W, LANES), F32),
                        pltpu.VMEM((FOX_HEADS, 1), F32), pltpu.VMEM((FOX_HEADS, 1), F32),
                        pltpu.VMEM((GROUP_W, LANES), F32),
                        pltpu.VMEM((FOX_HEADS, 1), F32), pltpu.VMEM((FOX_HEADS, 1), F32),
                        pltpu.VMEM((FOX_HEADS, LANES), F32),
                        pltpu.VMEM((FOX_HEADS, 1), F32)],
    )
    return pl.pallas_call(
        _decode_kernel,
        grid_spec=grid_spec,
        out_shape=jax.ShapeDtypeStruct((nb, 1, MIX_W), BF16),
        compiler_params=_params(2),
        name="decode",
    )(page_table, fq16, dq16, fk_new, fv_new, dk_new, dv_new, lf_col,
      *([cache_kt] * pp), *([cache_vt] * pp), *([cache_lf_t] * pp),
      *([cache_dk] * pp), *([cache_dv] * pp), *lams, g_fox, g_diff)


def _topk_rows(s, order, payload=None):
    big = jnp.int32(2 ** 30)
    vals, picks = [], []
    for _ in range(PEER_TOPK):
        m = jnp.max(s, axis=0, keepdims=True)
        r = jnp.min(jnp.where(s == m, order, big), axis=0, keepdims=True)
        hit = order == r
        vals.append(m)
        if payload is None:
            picks.append(r)
        else:
            picks.append(jnp.sum(jnp.where(hit, payload, 0), axis=0, keepdims=True))
        s = jnp.where(hit, NEG_INF, s)
    return jnp.concatenate(vals, axis=0), jnp.concatenate(picks, axis=0)


def _mix_kernel(x_ref, mixed_ref, wo_ref, g_ref, wpq_ref, k1_ref, k2_ref,
                x2_ref, h2_ref, idx_ref, gate_ref):
    tm = x_ref.shape[1]
    x2 = x_ref[0] + _dot(mixed_ref[0], wo_ref[...])
    x2_ref[0] = x2
    h16 = _rms(x2, g_ref[...]).astype(BF16)
    h2_ref[0] = h16.astype(F32)
    key_rank = lax.broadcasted_iota(I32, (PEER_KEYS, tm), 0)
    n_b = [PEER_TOPK // (a + 1) for a in range(PEER_TOPK)]
    rank16 = lax.broadcasted_iota(I32, (PEER_TOPK, tm), 0)
    cand_rank = jnp.concatenate([rank16[:n_b[a]] + PEER_TOPK * a for a in range(PEER_TOPK)], axis=0)
    for hd in range(PEER_HEADS):
        q = _dot(h16, wpq_ref[:, 2 * LANES * hd:2 * LANES * (hd + 1)])
        s1 = _dot_t(k1_ref[...], q[:, :LANES].astype(BF16))
        s2 = _dot_t(k2_ref[...], q[:, LANES:].astype(BF16))
        v1, i1 = _topk_rows(s1, key_rank)
        v2, i2 = _topk_rows(s2, key_rank)
        cand = jnp.concatenate([v1[a:a + 1] + v2[:n_b[a]] for a in range(PEER_TOPK)], axis=0)
        cidx = jnp.concatenate([i1[a:a + 1] * PEER_KEYS + i2[:n_b[a]] for a in range(PEER_TOPK)],
                               axis=0)
        sc, idx = _topk_rows(cand, cand_rank, cidx)
        e = jnp.exp(sc - sc[0:1])
        gate = e / jnp.sum(e, axis=0, keepdims=True)
        idx_ref[0, PEER_TOPK * hd:PEER_TOPK * (hd + 1), :] = idx * HALF_ROWS
        gate_ref[0, PEER_TOPK * hd:PEER_TOPK * (hd + 1), :] = gate


def _mix(x, mixed, w_o16, g_ffn, w_pq16, k1_16, k2_16, tm):
    n, rows, _ = x.shape
    grid = (n, rows // tm)
    row_blk = lambda w: pl.BlockSpec((1, tm, w), lambda b, j: (b, j, 0))
    col_blk = pl.BlockSpec((1, PEER_PICKS, tm), lambda b, j: (b, 0, j))
    const = lambda shape: pl.BlockSpec(shape, lambda b, j: (0,) * len(shape))
    return pl.pallas_call(
        _mix_kernel,
        grid=grid,
        in_specs=[row_blk(D_MODEL), row_blk(MIX_W), const((MIX_W, D_MODEL)), const((1, D_MODEL)),
                  const((D_MODEL, 2 * LANES * PEER_HEADS)), const((PEER_KEYS, LANES)),
                  const((PEER_KEYS, LANES))],
        out_specs=[row_blk(D_MODEL), row_blk(D_MODEL), col_blk, col_blk],
        out_shape=[jax.ShapeDtypeStruct((n, rows, D_MODEL), F32),
                   jax.ShapeDtypeStruct((n, rows, D_MODEL), F32),
                   jax.ShapeDtypeStruct((n, PEER_PICKS, rows), I32),
                   jax.ShapeDtypeStruct((n, PEER_PICKS, rows), F32)],
        compiler_params=_params(2),
        name="mix",
    )(x, mixed, w_o16, g_ffn, w_pq16, k1_16, k2_16)


HI_MASK = -65536
TOKEN_UNROLL = 8


def _unrolled_loop(n, body):
    def chunk(c, _):
        for u in range(TOKEN_UNROLL):
            body(c * TOKEN_UNROLL + u, 0)
        return 0

    lax.fori_loop(0, n // TOKEN_UNROLL, chunk, 0)


def _red_stride(tb):
    return tb * PEER_PICKS + 8


def _unpack(word):
    hi = pltpu.bitcast(word & HI_MASK, F32)
    lo = pltpu.bitcast(word << 16, F32)
    return hi, lo


def _expert_row(tab_ref, row):
    return tab_ref[pl.ds(pl.multiple_of(row, HALF_ROWS), HALF_ROWS), :]


def _peer_u_kernel(idx_ref, x_ref, gate_ref, tab_ref, w_ref, red, act):
    tb = gate_ref.shape[0]
    stride = _red_stride(tb)

    def gather(t, _):
        row = x_ref[pl.ds(t, 1), :]
        chunk = lambda r: row[:, LANES * r:LANES * (r + 1)]
        xa = jnp.concatenate([chunk(r) for r in range(HALF_ROWS)], axis=0)
        xb = jnp.concatenate([chunk(HALF_ROWS + r) for r in range(HALF_ROWS)], axis=0)
        picks = idx_ref.at[t]
        base = pl.multiple_of(t * PEER_PICKS, PEER_PICKS)
        window = red.at[pl.ds(base, (HALF_ROWS - 1) * stride + PEER_PICKS)]
        for j in range(PEER_PICKS):
            hi, lo = _unpack(_expert_row(tab_ref, picks[j]))
            window[pl.ds(j, HALF_ROWS, stride=stride), :] = hi * xa + lo * xb
        return 0

    def reduce(t, _):
        base = pl.multiple_of(t * PEER_PICKS, PEER_PICKS)
        blk = lambda s: red[pl.ds(base + s * stride, PEER_PICKS), :]
        part = (blk(0) + blk(1)) + (blk(2) + blk(3))
        act[pl.ds(t, 1), :] = jnp.sum(part.T, axis=0, keepdims=True)
        return 0

    lax.fori_loop(0, tb, gather, 0)
    _unrolled_loop(tb, reduce)
    a = act[...]
    w = gate_ref[...] * (0.5 * a * (1.0 + lax.erf(a * (2.0 ** -0.5))))
    w_ref[...] = w.astype(BF16).astype(F32)


def _peer_v_kernel(idx_ref, w_ref, tab_ref, x2_ref, g_ref, y_ref, wb, peer):
    tb = idx_ref.shape[0]
    n_acc = 2

    def spread(t, _):
        wb[pl.ds(pl.multiple_of(t * PEER_PICKS, PEER_PICKS), PEER_PICKS), :] = (
            jnp.broadcast_to(w_ref[pl.ds(t, 1), :], (PEER_PICKS, LANES)).T)
        return 0

    _unrolled_loop(tb, spread)

    def token(t, _):
        acc_a = [jnp.zeros((HALF_ROWS, LANES), F32) for _ in range(n_acc)]
        acc_b = [jnp.zeros((HALF_ROWS, LANES), F32) for _ in range(n_acc)]
        picks = idx_ref.at[t]
        wt = wb.at[pl.ds(pl.multiple_of(t * PEER_PICKS, PEER_PICKS), PEER_PICKS)]
        for j in range(PEER_PICKS):
            hi, lo = _unpack(_expert_row(tab_ref, picks[j]))
            w = jnp.broadcast_to(wt[j:j + 1, :], (HALF_ROWS, LANES))
            acc_a[j % n_acc] = acc_a[j % n_acc] + w * hi
            acc_b[j % n_acc] = acc_b[j % n_acc] + w * lo
        a = functools.reduce(lambda u, v: u + v, acc_a)
        b = functools.reduce(lambda u, v: u + v, acc_b)
        peer[pl.ds(t, 1), :] = jnp.concatenate(
            [a[r:r + 1] for r in range(HALF_ROWS)] + [b[r:r + 1] for r in range(HALF_ROWS)], axis=1)
        return 0

    lax.fori_loop(0, tb, token, 0)
    y_ref[...] = _rms(x2_ref[...] + peer[...], g_ref[...])


def _resident(shape):
    return pl.BlockSpec(shape, lambda *_: (0,) * len(shape), pipeline_mode=pl.Buffered(1))


def _peer_u(idx, h2, gate, table, tb):
    n, rows, _ = idx.shape
    grid = (n, rows // tb)
    return pl.pallas_call(
        _peer_u_kernel,
        grid=grid,
        in_specs=[pl.BlockSpec((None, tb, PEER_PICKS), lambda b, j: (b, j, 0),
                               memory_space=pltpu.SMEM),
                  pl.BlockSpec((None, tb, D_MODEL), lambda b, j: (b, j, 0)),
                  pl.BlockSpec((None, tb, PEER_PICKS), lambda b, j: (b, j, 0)),
                  _resident(table.shape)],
        out_specs=pl.BlockSpec((None, tb, PEER_PICKS), lambda b, j: (b, j, 0)),
        out_shape=jax.ShapeDtypeStruct((n, rows, PEER_PICKS), F32),
        scratch_shapes=[pltpu.VMEM((HALF_ROWS * _red_stride(tb), LANES), F32),
                        pltpu.VMEM((tb, PEER_PICKS), F32)],
        compiler_params=_params(2),
        name="peer_u",
    )(idx, h2, gate, table)


def _peer_v(idx, w, table, x2, g_final, tb):
    n, rows, _ = idx.shape
    grid = (n, rows // tb)
    row_blk = lambda width: pl.BlockSpec((None, tb, width), lambda b, j: (b, j, 0))
    return pl.pallas_call(
        _peer_v_kernel,
        grid=grid,
        in_specs=[pl.BlockSpec((None, tb, PEER_PICKS), lambda b, j: (b, j, 0),
                               memory_space=pltpu.SMEM),
                  row_blk(PEER_PICKS), _resident(table.shape), row_blk(D_MODEL),
                  pl.BlockSpec((1, D_MODEL), lambda b, j: (0, 0))],
        out_specs=row_blk(D_MODEL),
        out_shape=jax.ShapeDtypeStruct((n, rows, D_MODEL), F32),
        scratch_shapes=[pltpu.VMEM((tb * PEER_PICKS, LANES), F32),
                        pltpu.VMEM((tb, D_MODEL), F32)],
        compiler_params=_params(2),
        name="peer_v",
    )(idx, w, table, x2, g_final)


def _pack_table(t):
    t16 = t.astype(BF16)
    hi = lax.bitcast_convert_type(t16[:, :GROUP_W], jnp.uint16).astype(jnp.uint32)
    lo = lax.bitcast_convert_type(t16[:, GROUP_W:], jnp.uint16).astype(jnp.uint32)
    packed = lax.bitcast_convert_type((hi << 16) | lo, I32)
    return packed.reshape(t.shape[0] * HALF_ROWS, LANES)


def _rope_tables(pos):
    half = HEAD_DIM // 2
    inv = ROPE_THETA ** (-jnp.arange(half, dtype=F32) / half)
    ang = pos.astype(F32)[:, None] * inv[None, :]
    cos, sin = jnp.cos(ang), jnp.sin(ang)
    zero = jnp.zeros_like(sin)
    return (jnp.concatenate([cos, cos, cos, cos], axis=1),
            jnp.concatenate([-sin, zero, -sin, zero], axis=1),
            jnp.concatenate([zero, sin, zero, sin], axis=1))


def kernel(x_prompt, x_sample, cache_fox_k, cache_fox_v, cache_fox_logf, cache_diff_k, cache_diff_v, page_table, meta_tokens, g_attn, w_in, b_f, lam_q1, lam_k1, lam_q2, lam_k2, g_fox, g_diff, w_o, g_ffn, w_pq, sub_k1, sub_k2, expert_u, expert_v, g_final):
    assert w_in.shape[0] == 1, "single-layer trunk"
    nb, seq, _ = x_prompt.shape
    db = x_sample.shape[0]
    n_pool = cache_fox_k.shape[1]
    n_pages = page_table.shape[1]
    lp_true = seq + N_META
    lp = -(-lp_true // Q_TILE) * Q_TILE
    peer_tb = 48

    w = w_in[0]
    c = [0, 512, 1024, 1536, 1544, 2056, 2568, 3080]
    w_re = jnp.concatenate([w[:, c[0]:c[3]], w[:, c[4]:c[7]], w[:, c[3]:c[4]],
                            jnp.zeros((D_MODEL, LANES - FOX_HEADS), F32)], axis=1).astype(BF16)
    bf_pad = jnp.concatenate([b_f[0], jnp.zeros((LANES - FOX_HEADS,), F32)])[None]
    g_attn2, g_ffn2, g_fox2, g_diff2 = g_attn[0][None], g_ffn[0][None], g_fox[0][None], g_diff[0][None]
    lams = (lam_q1[0][None], lam_k1[0][None], lam_q2[0][None], lam_k2[0][None])
    w_o16 = w_o[0].astype(BF16)
    w_pq16 = w_pq[0].astype(BF16)
    k1_16 = sub_k1[0].astype(BF16)
    k2_16 = sub_k2[0].astype(BF16)
    u_tab = _pack_table(expert_u[0])
    v_tab = _pack_table(expert_v[0])
    g_fin = g_final[None]

    xp = jnp.concatenate([jnp.broadcast_to(meta_tokens[None], (nb, N_META, D_MODEL)), x_prompt,
                          jnp.zeros((nb, lp - lp_true, D_MODEL), F32)], axis=1)
    cos, sa, sb = _rope_tables(jnp.arange(lp))
    (fq16, fk32, fv32, dq16, dk32, dv32, lf, fk16, fvt16, dk16, dvt16) = _inproj(
        xp, g_attn2, w_re, bf_pad, cos, sa, sb, lp_true, Q_TILE, True)

    lf_t = jnp.pad(jnp.swapaxes(lf, 1, 2), ((0, 0), (0, 0), (0, lp - lp_true)))
    cum_t = _cumsum_lanes(lf_t.reshape(nb * FOX_HEADS, lp)).reshape(nb, FOX_HEADS, lp)
    cum = jnp.swapaxes(cum_t, 1, 2)
    mixed = _attention(fq16, dq16, fk16, fvt16, dk16, dvt16, cum, cum_t, lams, g_fox2, g_diff2)

    x2, h2, idx_t, gate_t = _mix(xp, mixed, w_o16, g_ffn2, w_pq16, k1_16, k2_16, Q_TILE)
    idx = jnp.swapaxes(idx_t, 1, 2)[:, :lp_true]
    gate = jnp.swapaxes(gate_t, 1, 2)[:, :lp_true]
    wts = _peer_u(idx, h2, gate, u_tab, peer_tb)
    y_prompt = _peer_v(idx, wts, v_tab, x2, g_fin, peer_tb)[:, N_META:]

    cos_s, sa_s, sb_s = _rope_tables(jnp.full((db,), n_pages * PAGE))
    xs = x_sample.reshape(1, db, D_MODEL)
    (sfq16, sfk32, sfv32, sdq16, sdk32, sdv32, slf) = _inproj(
        xs, g_attn2, w_re, bf_pad, cos_s, sa_s, sb_s, db, db, False)
    tok = lambda a: a.reshape(db, 1, a.shape[-1])
    cache_kt = jnp.transpose(cache_fox_k[0], (0, 2, 3, 1))
    cache_vt = jnp.transpose(cache_fox_v[0], (0, 2, 3, 1))
    cache_lf_t = jnp.swapaxes(cache_fox_logf[0], 1, 2)
    cache_dk = cache_diff_k[0].reshape(n_pool, PAGE * 4, LANES)
    cache_dv = cache_diff_v[0].reshape(n_pool, PAGE * 4, LANES)
    mixed_s = _decode(page_table, tok(sfq16), tok(sdq16), tok(sfk32), tok(sfv32), tok(sdk32),
                      tok(sdv32), slf.reshape(db, FOX_HEADS, 1),
                      cache_kt, cache_vt, cache_lf_t, cache_dk, cache_dv, lams, g_fox2, g_diff2)
    sx2, sh2, sidx_t, sgate_t = _mix(xs, mixed_s.reshape(1, db, MIX_W), w_o16, g_ffn2, w_pq16,
                                     k1_16, k2_16, db)
    sidx = jnp.swapaxes(sidx_t, 1, 2)
    sgate = jnp.swapaxes(sgate_t, 1, 2)
    swts = _peer_u(sidx, sh2, sgate, u_tab, db)
    y_sample = _peer_v(sidx, swts, v_tab, sx2, g_fin, db).reshape(db, 1, D_MODEL)

    heads = lambda a, h: a.reshape((1, a.shape[0], a.shape[1], h, a.shape[2] // h))
    sheads = lambda a, h: a.reshape((1, db, 1, h, a.shape[-1] // h))
    return (y_prompt, y_sample,
            heads(fk32, 8), heads(fv32, 8), lf[None], heads(dk32, 4), heads(dv32, 4),
            sheads(sfk32, 8), sheads(sfv32, 8), slf.reshape(1, db, 1, FOX_HEADS),
            sheads(sdk32, 4), sheads(sdv32, 4))
```

```python
import functools

import jax
import jax.numpy as jnp
from jax import lax
from jax.experimental import pallas as pl
from jax.experimental.pallas import tpu as pltpu

F32 = jnp.float32
BF16 = jnp.bfloat16
I32 = jnp.int32

D_MODEL = 1024
N_META = 16
PAGE = 128
FOX_HEADS = 8
HEAD_DIM = 64
GROUP_W = 512
MIX_W = 1024
ROPE_THETA = 10000.0
RMS_EPS = 1e-6
QK_SCALE = 0.125
LAM_INIT = 0.2
PEER_HEADS = 8
PEER_KEYS = 128
PEER_TOPK = 16
PEER_PICKS = PEER_HEADS * PEER_TOPK
LANES = 128
Q_TILE = 128
DEC_PAGES = 8
N_CHAINS = 16
HALF_ROWS = 4
NEG_INF = float("-inf")
W_IN_COLS = 6 * GROUP_W + LANES

VMEM_LIMIT = 56 * 1024 * 1024


def _params(n_grid, vmem=VMEM_LIMIT):
    return pltpu.CompilerParams(dimension_semantics=("arbitrary",) * n_grid,
                                vmem_limit_bytes=vmem)


def _rms(x, g):
    return x * lax.rsqrt(jnp.mean(x * x, axis=-1, keepdims=True) + RMS_EPS) * g


def _dot_t(a, b):
    return lax.dot_general(a, b, (((1,), (1,)), ((), ())), preferred_element_type=F32)


def _dot(a, b):
    return jnp.dot(a, b, preferred_element_type=F32)


def _lambda(lq1, lk1, lq2, lk2):
    return (jnp.exp(jnp.sum(lq1[...] * lk1[...], axis=1, keepdims=True))
            - jnp.exp(jnp.sum(lq2[...] * lk2[...], axis=1, keepdims=True)) + LAM_INIT)


def _inproj_kernel(x_ref, g_ref, w_ref, bf_ref, cos_ref, sa_ref, sb_ref,
                   fq16, fk32, fv32, dq16, dk32, dv32, lf_ref, *attn_copies):
    x = x_ref[0]
    h16 = _rms(x, g_ref[...]).astype(BF16)

    def seg(k, width=GROUP_W):
        return _dot(h16, w_ref[:, GROUP_W * k:GROUP_W * k + width])

    cos = jnp.concatenate([cos_ref[...]] * 4, axis=1)
    sa = jnp.concatenate([sa_ref[...]] * 4, axis=1)
    sb = jnp.concatenate([sb_ref[...]] * 4, axis=1)

    def rope(v):
        return (v * cos + pltpu.roll(v, GROUP_W - HEAD_DIM // 2, axis=1) * sa
                + pltpu.roll(v, HEAD_DIM // 2, axis=1) * sb)

    fq16[0] = (seg(0) * QK_SCALE).astype(BF16)
    fk = seg(1)
    fk32[0] = fk
    fv = seg(2)
    fv32[0] = fv
    dq16[0] = (rope(seg(3)) * QK_SCALE).astype(BF16)
    dk = rope(seg(4))
    dk32[0] = dk
    dv = seg(5)
    dv32[0] = dv
    ff = seg(6, LANES) + bf_ref[...]
    lf_ref[0] = (jnp.minimum(ff, 0.0) - jnp.log1p(jnp.exp(-jnp.abs(ff))))[:, :FOX_HEADS]
    if attn_copies:
        fk16, fvt16, dk16, dvt16 = attn_copies
        fk16[0] = fk.astype(BF16)
        dk16[0] = dk.astype(BF16)
        fvt16[0] = fv.T.astype(BF16)
        dvt16[0] = dv.T.astype(BF16)


def _inproj(x, g_attn, w_re, bf_pad, cos, sa, sb, rows_out, tm, attn_copies):
    n, rows_in, _ = x.shape
    grid = (n, rows_in // tm)
    row_blk = lambda w: pl.BlockSpec((1, tm, w), lambda b, j: (b, j, 0))
    col_blk = pl.BlockSpec((1, GROUP_W, tm), lambda b, j: (b, 0, j))
    const = lambda shape: pl.BlockSpec(shape, lambda b, j: (0,) * len(shape))
    tab = pl.BlockSpec((tm, LANES), lambda b, j: (j, 0))
    o16 = jax.ShapeDtypeStruct((n, rows_in, GROUP_W), BF16)
    o16t = jax.ShapeDtypeStruct((n, GROUP_W, rows_in), BF16)
    o32 = jax.ShapeDtypeStruct((n, rows_out, GROUP_W), F32)
    out_specs = [row_blk(GROUP_W)] * 6 + [row_blk(FOX_HEADS)]
    out_shape = [o16, o32, o32, o16, o32, o32, jax.ShapeDtypeStruct((n, rows_out, FOX_HEADS), F32)]
    if attn_copies:
        out_specs += [row_blk(GROUP_W), col_blk, row_blk(GROUP_W), col_blk]
        out_shape += [o16, o16t, o16, o16t]
    return pl.pallas_call(
        _inproj_kernel,
        grid=grid,
        in_specs=[row_blk(D_MODEL), const((1, D_MODEL)), const((D_MODEL, W_IN_COLS)),
                  const((1, LANES)), tab, tab, tab],
        out_specs=out_specs,
        out_shape=out_shape,
        compiler_params=_params(2),
        name="inproj",
    )(x, g_attn, w_re, bf_pad, cos, sa, sb)


def _cumsum_kernel(x_ref, o_ref):
    rows, cols = x_ref.shape
    lane = lax.broadcasted_iota(I32, (rows, LANES), 1)
    carry = jnp.zeros((rows, 1), F32)
    for c in range(cols // LANES):
        x = x_ref[:, LANES * c:LANES * (c + 1)]
        s = 1
        while s < LANES:
            x = x + jnp.where(lane >= s, pltpu.roll(x, s, axis=1), 0.0)
            s *= 2
        x = x + carry
        o_ref[:, LANES * c:LANES * (c + 1)] = x
        carry = x[:, LANES - 1:LANES]


def _cumsum_lanes(x):
    return pl.pallas_call(
        _cumsum_kernel,
        out_shape=jax.ShapeDtypeStruct(x.shape, F32),
        name="cumsum",
    )(x)


def _attn_kernel(fq_ref, dq_ref, fk_ref, fvt_ref, dk_ref, dvt_ref, cum_ref, cumt_ref,
                 lq1, lk1, lq2, lk2, gfox_ref, gdiff_ref, o_ref, ckb, qh, s_sc, p_sc, *accs):
    t = pl.program_id(1)
    lp = fk_ref.shape[1]

    @pl.when(t == 0)
    def _():
        for h in range(FOX_HEADS):
            ckb[h] = jnp.broadcast_to(cum_ref[0, :, h:h + 1], (lp, LANES))

    key = lax.broadcasted_iota(I32, (Q_TILE, Q_TILE), 0)
    qry = lax.broadcasted_iota(I32, (Q_TILE, Q_TILE), 1)
    lane_lo = qry < HEAD_DIM
    qstart = pl.multiple_of(t * Q_TILE, Q_TILE)

    chains = []
    for kind, (q_ref, k_ref, vt_ref) in enumerate(((fq_ref, fk_ref, fvt_ref), (dq_ref, dk_ref, dvt_ref))):
        for g in range(4):
            q = q_ref[0, :, LANES * g:LANES * (g + 1)].astype(F32)
            for half in range(2):
                c = 8 * kind + 2 * g + half
                keep = lane_lo if half == 0 else jnp.logical_not(lane_lo)
                qh[c] = jnp.where(keep, q, 0.0).astype(BF16)
                v_rows = (LANES * g + HEAD_DIM * half, HEAD_DIM) if kind == 0 else (LANES * g, LANES)
                accs[c][...] = jnp.zeros(accs[c].shape, F32)
                chains.append((c, kind, g, 2 * g + half, k_ref, vt_ref, v_rows))

    chain_row = lax.broadcasted_iota(I32, (N_CHAINS, Q_TILE), 0)

    def body(j, carry, diagonal=False):
        m_all, l_all = carry
        start = pl.multiple_of(j * Q_TILE, Q_TILE)
        for c, kind, g, h, k_ref, vt_ref, (r0, nr) in chains:
            k = k_ref[0, pl.ds(start, Q_TILE), LANES * g:LANES * (g + 1)]
            s_sc[c] = _dot_t(k, qh[c])
        alphas = []
        for c, kind, g, h, k_ref, vt_ref, (r0, nr) in chains:
            s = s_sc[c]
            if kind == 0:
                cq = cumt_ref[0, h:h + 1, pl.ds(qstart, Q_TILE)]
                s = s + (cq - ckb[h, pl.ds(start, Q_TILE), :])
            if diagonal:
                s = jnp.where(key <= qry, s, NEG_INF)
            m = m_all[c:c + 1]
            m_new = jnp.maximum(m, jnp.max(s, axis=0, keepdims=True))
            alpha = jnp.exp(m - m_new)
            p = jnp.exp(s - m_new)
            l_new = alpha * l_all[c:c + 1] + jnp.sum(p, axis=0, keepdims=True)
            m_all = jnp.where(chain_row == c, m_new, m_all)
            l_all = jnp.where(chain_row == c, l_new, l_all)
            p_sc[c] = p.astype(BF16)
            alphas.append(alpha)
        for c, kind, g, h, k_ref, vt_ref, (r0, nr) in chains:
            vt = vt_ref[0, r0:r0 + nr, pl.ds(start, Q_TILE)]
            accs[c][...] = alphas[c] * accs[c][...] + _dot(vt, p_sc[c])
        return m_all, l_all

    init = (jnp.full((N_CHAINS, Q_TILE), NEG_INF, F32), jnp.zeros((N_CHAINS, Q_TILE), F32))
    _, l_all = body(t, lax.fori_loop(0, t, body, init), diagonal=True)

    out = [accs[c][...] / l_all[c:c + 1] for c in range(N_CHAINS)]
    fox = [jnp.concatenate([out[2 * g], out[2 * g + 1]], axis=0).T for g in range(4)]
    o_f = _rms(jnp.concatenate(fox, axis=1), gfox_ref[...])
    lam = _lambda(lq1, lk1, lq2, lk2)
    diff = []
    for g in range(4):
        od = (out[8 + 2 * g] - lam * out[8 + 2 * g + 1]).T
        diff.append(_rms(od, gdiff_ref[...]) * (1.0 - LAM_INIT))
    o_ref[0] = jnp.concatenate([o_f] + diff, axis=1).astype(BF16)


def _attention(fq16, dq16, fk16, fvt16, dk16, dvt16, cum, cum_t, lams, g_fox, g_diff):
    n, lp, _ = fq16.shape
    grid = (n, lp // Q_TILE)
    tile = lambda w: pl.BlockSpec((1, Q_TILE, w), lambda b, t: (b, t, 0))
    full = lambda r, w: pl.BlockSpec((1, r, w), lambda b, t: (b, 0, 0))
    const = lambda shape: pl.BlockSpec(shape, lambda b, t: (0,) * len(shape))
    return pl.pallas_call(
        _attn_kernel,
        grid=grid,
        in_specs=[tile(GROUP_W), tile(GROUP_W), full(lp, GROUP_W), full(GROUP_W, lp),
                  full(lp, GROUP_W), full(GROUP_W, lp), full(lp, FOX_HEADS), full(FOX_HEADS, lp)]
                 + [const((1, HEAD_DIM))] * 4 + [const((1, GROUP_W)), const((1, LANES))],
        out_specs=tile(MIX_W),
        out_shape=jax.ShapeDtypeStruct((n, lp, MIX_W), BF16),
        scratch_shapes=[pltpu.VMEM((FOX_HEADS, lp, LANES), F32),
                        pltpu.VMEM((N_CHAINS, Q_TILE, LANES), BF16),
                        pltpu.VMEM((N_CHAINS, Q_TILE, Q_TILE), F32),
                        pltpu.VMEM((N_CHAINS, Q_TILE, Q_TILE), BF16)]
                       + [pltpu.VMEM((HEAD_DIM, Q_TILE), F32)] * 8
                       + [pltpu.VMEM((LANES, Q_TILE), F32)] * 8,
        compiler_params=_params(2),
        name="attention",
    )(fq16, dq16, fk16, fvt16, dk16, dvt16, cum, cum_t, *lams, g_fox, g_diff)


def _col_bcast(row):
    return jnp.concatenate(
        [jnp.broadcast_to(row[:, LANES * g:LANES * (g + 1)], (LANES, LANES)).T for g in range(4)],
        axis=0)


def _row_from_col(col):
    return jnp.concatenate(
        [jnp.broadcast_to(col[LANES * g:LANES * (g + 1)], (LANES, LANES)).T[0:1] for g in range(4)],
        axis=1)


def _decode_kernel(pt_ref, fq_ref, dq_ref, fkn_ref, fvn_ref, dkn_ref, dvn_ref, lfn_ref, *refs):
    pp = DEC_PAGES
    kt, vt, lft, dk, dv = (refs[i * pp:(i + 1) * pp] for i in range(5))
    (lq1, lk1, lq2, lk2, gfox_ref, gdiff_ref, o_ref,
     qb, mf, lf, af, md, ld, ad, carry) = refs[5 * pp:]
    p = pl.program_id(1)
    heads = FOX_HEADS

    def head_rows(row):
        return jnp.concatenate([row[:, LANES * (j // 2):LANES * (j // 2 + 1)] for j in range(heads)],
                               axis=0)

    sub = lax.broadcasted_iota(I32, (heads, LANES), 0)
    lane = lax.broadcasted_iota(I32, (heads, LANES), 1)
    qd = jnp.where((lane // HEAD_DIM) == (sub % 2), head_rows(dq_ref[0].astype(F32)), 0.0)

    @pl.when(p == 0)
    def _():
        fq = fq_ref[0].astype(F32)
        qb[...] = _col_bcast(fq)
        hrow = lax.broadcasted_iota(I32, (heads, GROUP_W), 0)
        hlane = lax.broadcasted_iota(I32, (heads, GROUP_W), 1)
        own = (hlane // HEAD_DIM) == hrow
        mf[...] = jnp.sum(jnp.where(own, fq * fkn_ref[0], 0.0), axis=1, keepdims=True)
        lf[...] = jnp.ones_like(lf)
        lane0 = lax.broadcasted_iota(I32, (GROUP_W, LANES), 1) == 0
        af[...] = jnp.where(lane0, _col_bcast(fvn_ref[0]), 0.0)
        md[...] = jnp.sum(qd * head_rows(dkn_ref[0]), axis=1, keepdims=True)
        ld[...] = jnp.ones_like(ld)
        ad[...] = head_rows(dvn_ref[0])
        carry[...] = lfn_ref[0]

    c = carry[...]
    scores = []
    for i in range(pp):
        x = lft[i][0]
        incl = x
        step = 1
        while step < PAGE:
            incl = incl + jnp.where(lane < PAGE - step, pltpu.roll(incl, PAGE - step, axis=1), 0.0)
            step *= 2
        bias = (incl - x) + c
        c = c + incl[:, 0:1]
        rows = [jnp.sum(kt[i][0, h] * qb[HEAD_DIM * h:HEAD_DIM * (h + 1), :], axis=0, keepdims=True)
                for h in range(heads)]
        scores.append(jnp.concatenate(rows, axis=0) + bias)
    carry[...] = c
    m_old = mf[...]
    m_new = m_old
    for s in scores:
        m_new = jnp.maximum(m_new, jnp.max(s, axis=1, keepdims=True))
    alpha = jnp.exp(m_old - m_new)
    probs = [jnp.exp(s - m_new) for s in scores]
    l_new = alpha * lf[...]
    for pr in probs:
        l_new = l_new + jnp.sum(pr, axis=1, keepdims=True)
    lf[...] = l_new
    mf[...] = m_new
    for h in range(heads):
        rows_h = slice(HEAD_DIM * h, HEAD_DIM * (h + 1))
        acc = af[rows_h, :] * jnp.broadcast_to(alpha[h:h + 1, :], (HEAD_DIM, LANES))
        for i in range(pp):
            acc = acc + jnp.broadcast_to(probs[i][h:h + 1, :], (HEAD_DIM, LANES)) * vt[i][0, h]
        af[rows_h, :] = acc

    qd16 = qd.astype(BF16)
    drow = lax.broadcasted_iota(I32, (heads, PAGE * 4), 0)
    dcol = lax.broadcasted_iota(I32, (heads, PAGE * 4), 1)
    visible = (dcol % 4) == (drow // 2)
    scores = [jnp.where(visible, _dot_t(qd16, dk[i][0].astype(BF16)), NEG_INF) for i in range(pp)]
    m_old = md[...]
    m_new = m_old
    for s in scores:
        m_new = jnp.maximum(m_new, jnp.max(s, axis=1, keepdims=True))
    alpha = jnp.exp(m_old - m_new)
    l_new = alpha * ld[...]
    acc = alpha * ad[...]
    for i in range(pp):
        pr = jnp.exp(scores[i] - m_new)
        l_new = l_new + jnp.sum(pr, axis=1, keepdims=True)
        acc = acc + _dot(pr.astype(BF16), dv[i][0].astype(BF16))
    ld[...] = l_new
    md[...] = m_new
    ad[...] = acc

    @pl.when(p == pl.num_programs(1) - 1)
    def _():
        o_col = jnp.sum(af[...], axis=1, keepdims=True)
        l_col = jnp.concatenate([jnp.broadcast_to(lf[h:h + 1, :], (HEAD_DIM, 1)) for h in range(heads)],
                                axis=0)
        o_f = _rms(_row_from_col(o_col / l_col), gfox_ref[...])
        lam = _lambda(lq1, lk1, lq2, lk2)
        od = ad[...] / ld[...]
        parts = [_rms(od[2 * h:2 * h + 1] - lam * od[2 * h + 1:2 * h + 2], gdiff_ref[...])
                 * (1.0 - LAM_INIT) for h in range(4)]
        o_ref[0] = jnp.concatenate([o_f] + parts, axis=1).astype(BF16)


def _decode(page_table, fq16, dq16, fk_new, fv_new, dk_new, dv_new, lf_col,
            cache_kt, cache_vt, cache_lf_t, cache_dk, cache_dv, lams, g_fox, g_diff):
    nb, n_pages = page_table.shape
    pp = DEC_PAGES
    tok = lambda w: pl.BlockSpec((1, 1, w), lambda b, p, pt: (b, 0, 0))
    const = lambda shape: pl.BlockSpec(shape, lambda b, p, pt: (0,) * len(shape))

    def pages(block):
        zeros = (0,) * (len(block) - 1)
        return [pl.BlockSpec(block, lambda b, p, pt, i=i: (pt[b, n_pages - 1 - (p * pp + i)],) + zeros)
                for i in range(pp)]

    grid_spec = pltpu.PrefetchScalarGridSpec(
        num_scalar_prefetch=1,
        grid=(nb, n_pages // pp),
        in_specs=[tok(GROUP_W)] * 6 + [pl.BlockSpec((1, FOX_HEADS, 1), lambda b, p, pt: (b, 0, 0))]
                 + pages((1, FOX_HEADS, HEAD_DIM, PAGE)) + pages((1, FOX_HEADS, HEAD_DIM, PAGE))
                 + pages((1, FOX_HEADS, PAGE)) + pages((1, PAGE * 4, LANES)) + pages((1, PAGE * 4, LANES))
                 + [const((1, HEAD_DIM))] * 4 + [const((1, GROUP_W)), const((1, LANES))],
        out_specs=pl.BlockSpec((1, 1, MIX_W), lambda b, p, pt: (b, 0, 0)),
        scratch_shapes=[pltpu.VMEM((GROUP_W, LANES), F32),
                        pltpu.VMEM((FOX_HEADS, 1), F32), pltpu.VMEM((FOX_HEADS, 1), F32),
                        pltpu.VMEM((GROUP_W, LANES), F32),
                        pltpu.VMEM((FOX_HEADS, 1), F32), pltpu.VMEM((FOX_HEADS, 1), F32),
                        pltpu.VMEM((FOX_HEADS, LANES), F32),
                        pltpu.VMEM((FOX_HEADS, 1), F32)],
    )
    return pl.pallas_call(
        _decode_kernel,
        grid_spec=grid_spec,
        out_shape=jax.ShapeDtypeStruct((nb, 1, MIX_W), BF16),
        compiler_params=_params(2),
        name="decode",
    )(page_table, fq16, dq16, fk_new, fv_new, dk_new, dv_new, lf_col,
      *([cache_kt] * pp), *([cache_vt] * pp), *([cache_lf_t] * pp),
      *([cache_dk] * pp), *([cache_dv] * pp), *lams, g_fox, g_diff)


def _topk_rows(s, order, payload=None):
    big = jnp.int32(2 ** 30)
    vals, picks = [], []
    for _ in range(PEER_TOPK):
        m = jnp.max(s, axis=0, keepdims=True)
        r = jnp.min(jnp.where(s == m, order, big), axis=0, keepdims=True)
        hit = order == r
        vals.append(m)
        if payload is None:
            picks.append(r)
        else:
            picks.append(jnp.sum(jnp.where(hit, payload, 0), axis=0, keepdims=True))
        s = jnp.where(hit, NEG_INF, s)
    return jnp.concatenate(vals, axis=0), jnp.concatenate(picks, axis=0)


def _mix_kernel(x_ref, mixed_ref, wo_ref, g_ref, wpq_ref, k1_ref, k2_ref,
                x2_ref, h2_ref, idx_ref, gate_ref):
    tm = x_ref.shape[1]
    x2 = x_ref[0] + _dot(mixed_ref[0], wo_ref[...])
    x2_ref[0] = x2
    h16 = _rms(x2, g_ref[...]).astype(BF16)
    h2_ref[0] = h16.astype(F32)
    key_rank = lax.broadcasted_iota(I32, (PEER_KEYS, tm), 0)
    n_b = [PEER_TOPK // (a + 1) for a in range(PEER_TOPK)]
    rank16 = lax.broadcasted_iota(I32, (PEER_TOPK, tm), 0)
    cand_rank = jnp.concatenate([rank16[:n_b[a]] + PEER_TOPK * a for a in range(PEER_TOPK)], axis=0)
    for hd in range(PEER_HEADS):
        q = _dot(h16, wpq_ref[:, 2 * LANES * hd:2 * LANES * (hd + 1)])
        s1 = _dot_t(k1_ref[...], q[:, :LANES].astype(BF16))
        s2 = _dot_t(k2_ref[...], q[:, LANES:].astype(BF16))
        v1, i1 = _topk_rows(s1, key_rank)
        v2, i2 = _topk_rows(s2, key_rank)
        cand = jnp.concatenate([v1[a:a + 1] + v2[:n_b[a]] for a in range(PEER_TOPK)], axis=0)
        cidx = jnp.concatenate([i1[a:a + 1] * PEER_KEYS + i2[:n_b[a]] for a in range(PEER_TOPK)],
                               axis=0)
        sc, idx = _topk_rows(cand, cand_rank, cidx)
        e = jnp.exp(sc - sc[0:1])
        gate = e / jnp.sum(e, axis=0, keepdims=True)
        idx_ref[0, PEER_TOPK * hd:PEER_TOPK * (hd + 1), :] = idx * HALF_ROWS
        gate_ref[0, PEER_TOPK * hd:PEER_TOPK * (hd + 1), :] = gate


def _mix(x, mixed, w_o16, g_ffn, w_pq16, k1_16, k2_16, tm):
    n, rows, _ = x.shape
    grid = (n, rows // tm)
    row_blk = lambda w: pl.BlockSpec((1, tm, w), lambda b, j: (b, j, 0))
    col_blk = pl.BlockSpec((1, PEER_PICKS, tm), lambda b, j: (b, 0, j))
    const = lambda shape: pl.BlockSpec(shape, lambda b, j: (0,) * len(shape))
    return pl.pallas_call(
        _mix_kernel,
        grid=grid,
        in_specs=[row_blk(D_MODEL), row_blk(MIX_W), const((MIX_W, D_MODEL)), const((1, D_MODEL)),
                  const((D_MODEL, 2 * LANES * PEER_HEADS)), const((PEER_KEYS, LANES)),
                  const((PEER_KEYS, LANES))],
        out_specs=[row_blk(D_MODEL), row_blk(D_MODEL), col_blk, col_blk],
        out_shape=[jax.ShapeDtypeStruct((n, rows, D_MODEL), F32),
                   jax.ShapeDtypeStruct((n, rows, D_MODEL), F32),
                   jax.ShapeDtypeStruct((n, PEER_PICKS, rows), I32),
                   jax.ShapeDtypeStruct((n, PEER_PICKS, rows), F32)],
        compiler_params=_params(2),
        name="mix",
    )(x, mixed, w_o16, g_ffn, w_pq16, k1_16, k2_16)


HI_MASK = -65536


def _red_stride(tb):
    return tb * PEER_PICKS + 8


def _unpack(word):
    hi = pltpu.bitcast(word & HI_MASK, F32)
    lo = pltpu.bitcast(word << 16, F32)
    return hi, lo


def _expert_row(tab_ref, row):
    return tab_ref[pl.ds(pl.multiple_of(row, HALF_ROWS), HALF_ROWS), :]


def _peer_u_kernel(idx_ref, x_ref, gate_ref, tab_ref, w_ref, red, act):
    tb = gate_ref.shape[0]
    stride = _red_stride(tb)

    def gather(t, _):
        row = x_ref[pl.ds(t, 1), :]
        chunk = lambda r: row[:, LANES * r:LANES * (r + 1)]
        xa = jnp.concatenate([chunk(r) for r in range(HALF_ROWS)], axis=0)
        xb = jnp.concatenate([chunk(HALF_ROWS + r) for r in range(HALF_ROWS)], axis=0)
        picks = idx_ref.at[t]
        base = pl.multiple_of(t * PEER_PICKS, PEER_PICKS)
        window = red.at[pl.ds(base, (HALF_ROWS - 1) * stride + PEER_PICKS)]
        for j in range(PEER_PICKS):
            hi, lo = _unpack(_expert_row(tab_ref, picks[j]))
            window[pl.ds(j, HALF_ROWS, stride=stride), :] = hi * xa + lo * xb
        return 0

    def reduce(t, _):
        base = pl.multiple_of(t * PEER_PICKS, PEER_PICKS)
        blk = lambda s: red[pl.ds(base + s * stride, PEER_PICKS), :]
        part = (blk(0) + blk(1)) + (blk(2) + blk(3))
        act[pl.ds(t, 1), :] = jnp.sum(part.T, axis=0, keepdims=True)
        return 0

    gather(0, 0)

    def step(t, _):
        reduce(t - 1, 0)
        gather(t, 0)
        return 0

    lax.fori_loop(1, tb, step, 0)
    reduce(tb - 1, 0)
    a = act[...]
    w = gate_ref[...] * (0.5 * a * (1.0 + lax.erf(a * (2.0 ** -0.5))))
    w_ref[...] = w.astype(BF16).astype(F32)


def _peer_v_kernel(idx_ref, w_ref, tab_ref, x2_ref, g_ref, y_ref, peer):
    tb = idx_ref.shape[0]
    n_acc = 2

    def spread(t):
        return jnp.broadcast_to(w_ref[pl.ds(t, 1), :], (PEER_PICKS, LANES)).T

    def token(t, wt):
        wt_next = spread(jnp.minimum(t + 1, tb - 1))
        acc_a = [jnp.zeros((HALF_ROWS, LANES), F32) for _ in range(n_acc)]
        acc_b = [jnp.zeros((HALF_ROWS, LANES), F32) for _ in range(n_acc)]
        picks = idx_ref.at[t]
        for j in range(PEER_PICKS):
            hi, lo = _unpack(_expert_row(tab_ref, picks[j]))
            w = jnp.broadcast_to(wt[j:j + 1, :], (HALF_ROWS, LANES))
            acc_a[j % n_acc] = acc_a[j % n_acc] + w * hi
            acc_b[j % n_acc] = acc_b[j % n_acc] + w * lo
        a = functools.reduce(lambda u, v: u + v, acc_a)
        b = functools.reduce(lambda u, v: u + v, acc_b)
        peer[pl.ds(t, 1), :] = jnp.concatenate(
            [a[r:r + 1] for r in range(HALF_ROWS)] + [b[r:r + 1] for r in range(HALF_ROWS)], axis=1)
        return wt_next

    lax.fori_loop(0, tb, token, spread(0))
    y_ref[...] = _rms(x2_ref[...] + peer[...], g_ref[...])


def _resident(shape):
    return pl.BlockSpec(shape, lambda *_: (0,) * len(shape), pipeline_mode=pl.Buffered(1))


def _peer_u(idx, h2, gate, table, tb):
    n, rows, _ = idx.shape
    grid = (n, rows // tb)
    return pl.pallas_call(
        _peer_u_kernel,
        grid=grid,
        in_specs=[pl.BlockSpec((None, tb, PEER_PICKS), lambda b, j: (b, j, 0),
                               memory_space=pltpu.SMEM),
                  pl.BlockSpec((None, tb, D_MODEL), lambda b, j: (b, j, 0)),
                  pl.BlockSpec((None, tb, PEER_PICKS), lambda b, j: (b, j, 0)),
                  _resident(table.shape)],
        out_specs=pl.BlockSpec((None, tb, PEER_PICKS), lambda b, j: (b, j, 0)),
        out_shape=jax.ShapeDtypeStruct((n, rows, PEER_PICKS), F32),
        scratch_shapes=[pltpu.VMEM((HALF_ROWS * _red_stride(tb), LANES), F32),
                        pltpu.VMEM((tb, PEER_PICKS), F32)],
        compiler_params=_params(2),
        name="peer_u",
    )(idx, h2, gate, table)


def _peer_v(idx, w, table, x2, g_final, tb):
    n, rows, _ = idx.shape
    grid = (n, rows // tb)
    row_blk = lambda width: pl.BlockSpec((None, tb, width), lambda b, j: (b, j, 0))
    return pl.pallas_call(
        _peer_v_kernel,
        grid=grid,
        in_specs=[pl.BlockSpec((None, tb, PEER_PICKS), lambda b, j: (b, j, 0),
                               memory_space=pltpu.SMEM),
                  row_blk(PEER_PICKS), _resident(table.shape), row_blk(D_MODEL),
                  pl.BlockSpec((1, D_MODEL), lambda b, j: (0, 0))],
        out_specs=row_blk(D_MODEL),
        out_shape=jax.ShapeDtypeStruct((n, rows, D_MODEL), F32),
        scratch_shapes=[pltpu.VMEM((tb, D_MODEL), F32)],
        compiler_params=_params(2),
        name="peer_v",
    )(idx, w, table, x2, g_final)


def _pack_table(t):
    t16 = t.astype(BF16)
    hi = lax.bitcast_convert_type(t16[:, :GROUP_W], jnp.uint16).astype(jnp.uint32)
    lo = lax.bitcast_convert_type(t16[:, GROUP_W:], jnp.uint16).astype(jnp.uint32)
    packed = lax.bitcast_convert_type((hi << 16) | lo, I32)
    return packed.reshape(t.shape[0] * HALF_ROWS, LANES)


def _rope_tables(pos):
    half = HEAD_DIM // 2
    inv = ROPE_THETA ** (-jnp.arange(half, dtype=F32) / half)
    ang = pos.astype(F32)[:, None] * inv[None, :]
    cos, sin = jnp.cos(ang), jnp.sin(ang)
    zero = jnp.zeros_like(sin)
    return (jnp.concatenate([cos, cos, cos, cos], axis=1),
            jnp.concatenate([-sin, zero, -sin, zero], axis=1),
            jnp.concatenate([zero, sin, zero, sin], axis=1))


def kernel(x_prompt, x_sample, cache_fox_k, cache_fox_v, cache_fox_logf, cache_diff_k, cache_diff_v, page_table, meta_tokens, g_attn, w_in, b_f, lam_q1, lam_k1, lam_q2, lam_k2, g_fox, g_diff, w_o, g_ffn, w_pq, sub_k1, sub_k2, expert_u, expert_v, g_final):
    assert w_in.shape[0] == 1, "single-layer trunk"
    nb, seq, _ = x_prompt.shape
    db = x_sample.shape[0]
    n_pool = cache_fox_k.shape[1]
    n_pages = page_table.shape[1]
    lp_true = seq + N_META
    lp = -(-lp_true // Q_TILE) * Q_TILE
    peer_tb = 48

    w = w_in[0]
    c = [0, 512, 1024, 1536, 1544, 2056, 2568, 3080]
    w_re = jnp.concatenate([w[:, c[0]:c[3]], w[:, c[4]:c[7]], w[:, c[3]:c[4]],
                            jnp.zeros((D_MODEL, LANES - FOX_HEADS), F32)], axis=1).astype(BF16)
    bf_pad = jnp.concatenate([b_f[0], jnp.zeros((LANES - FOX_HEADS,), F32)])[None]
    g_attn2, g_ffn2, g_fox2, g_diff2 = g_attn[0][None], g_ffn[0][None], g_fox[0][None], g_diff[0][None]
    lams = (lam_q1[0][None], lam_k1[0][None], lam_q2[0][None], lam_k2[0][None])
    w_o16 = w_o[0].astype(BF16)
    w_pq16 = w_pq[0].astype(BF16)
    k1_16 = sub_k1[0].astype(BF16)
    k2_16 = sub_k2[0].astype(BF16)
    u_tab = _pack_table(expert_u[0])
    v_tab = _pack_table(expert_v[0])
    g_fin = g_final[None]

    xp = jnp.concatenate([jnp.broadcast_to(meta_tokens[None], (nb, N_META, D_MODEL)), x_prompt,
                          jnp.zeros((nb, lp - lp_true, D_MODEL), F32)], axis=1)
    cos, sa, sb = _rope_tables(jnp.arange(lp))
    (fq16, fk32, fv32, dq16, dk32, dv32, lf, fk16, fvt16, dk16, dvt16) = _inproj(
        xp, g_attn2, w_re, bf_pad, cos, sa, sb, lp_true, Q_TILE, True)

    lf_t = jnp.pad(jnp.swapaxes(lf, 1, 2), ((0, 0), (0, 0), (0, lp - lp_true)))
    cum_t = _cumsum_lanes(lf_t.reshape(nb * FOX_HEADS, lp)).reshape(nb, FOX_HEADS, lp)
    cum = jnp.swapaxes(cum_t, 1, 2)
    mixed = _attention(fq16, dq16, fk16, fvt16, dk16, dvt16, cum, cum_t, lams, g_fox2, g_diff2)

    x2, h2, idx_t, gate_t = _mix(xp, mixed, w_o16, g_ffn2, w_pq16, k1_16, k2_16, Q_TILE)
    idx = jnp.swapaxes(idx_t, 1, 2)[:, :lp_true]
    gate = jnp.swapaxes(gate_t, 1, 2)[:, :lp_true]
    wts = _peer_u(idx, h2, gate, u_tab, peer_tb)
    y_prompt = _peer_v(idx, wts, v_tab, x2, g_fin, peer_tb)[:, N_META:]

    cos_s, sa_s, sb_s = _rope_tables(jnp.full((db,), n_pages * PAGE))
    xs = x_sample.reshape(1, db, D_MODEL)
    (sfq16, sfk32, sfv32, sdq16, sdk32, sdv32, slf) = _inproj(
        xs, g_attn2, w_re, bf_pad, cos_s, sa_s, sb_s, db, db, False)
    tok = lambda a: a.reshape(db, 1, a.shape[-1])
    cache_kt = jnp.transpose(cache_fox_k[0], (0, 2, 3, 1))
    cache_vt = jnp.transpose(cache_fox_v[0], (0, 2, 3, 1))
    cache_lf_t = jnp.swapaxes(cache_fox_logf[0], 1, 2)
    cache_dk = cache_diff_k[0].reshape(n_pool, PAGE * 4, LANES)
    cache_dv = cache_diff_v[0].reshape(n_pool, PAGE * 4, LANES)
    mixed_s = _decode(page_table, tok(sfq16), tok(sdq16), tok(sfk32), tok(sfv32), tok(sdk32),
                      tok(sdv32), slf.reshape(db, FOX_HEADS, 1),
                      cache_kt, cache_vt, cache_lf_t, cache_dk, cache_dv, lams, g_fox2, g_diff2)
    sx2, sh2, sidx_t, sgate_t = _mix(xs, mixed_s.reshape(1, db, MIX_W), w_o16, g_ffn2, w_pq16,
                                     k1_16, k2_16, db)
    sidx = jnp.swapaxes(sidx_t, 1, 2)
    sgate = jnp.swapaxes(sgate_t, 1, 2)
    swts = _peer_u(sidx, sh2, sgate, u_tab, db)
    y_sample = _peer_v(sidx, swts, v_tab, sx2, g_fin, db).reshape(db, 1, D_MODEL)

    heads = lambda a, h: a.reshape((1, a.shape[0], a.shape[1], h, a.shape[2] // h))
    sheads = lambda a, h: a.reshape((1, db, 1, h, a.shape[-1] // h))
    return (y_prompt, y_sample,
            heads(fk32, 8), heads(fv32, 8), lf[None], heads(dk32, 4), heads(dv32, 4),
            sheads(sfk32, 8), sheads(sfv32, 8), slf.reshape(1, db, 1, FOX_HEADS),
            sheads(sdk32, 4), sheads(sdv32, 4))
```

```python
import functools

import jax
import jax.numpy as jnp
from jax import lax
from jax.experimental import pallas as pl
from jax.experimental.pallas import tpu as pltpu

F32 = jnp.float32
BF16 = jnp.bfloat16
I32 = jnp.int32

D_MODEL = 1024
N_META = 16
PAGE = 128
FOX_HEADS = 8
HEAD_DIM = 64
GROUP_W = 512
MIX_W = 1024
ROPE_THETA = 10000.0
RMS_EPS = 1e-6
QK_SCALE = 0.125
LAM_INIT = 0.2
PEER_HEADS = 8
PEER_KEYS = 128
PEER_TOPK = 16
PEER_PICKS = PEER_HEADS * PEER_TOPK
LANES = 128
Q_TILE = 128
DEC_PAGES = 8
N_CHAINS = 16
HALF_ROWS = 4
NEG_INF = float("-inf")
W_IN_COLS = 6 * GROUP_W + LANES

VMEM_LIMIT = 56 * 1024 * 1024


def _params(n_grid, vmem=VMEM_LIMIT):
    return pltpu.CompilerParams(dimension_semantics=("arbitrary",) * n_grid,
                                vmem_limit_bytes=vmem)


def _rms(x, g):
    return x * lax.rsqrt(jnp.mean(x * x, axis=-1, keepdims=True) + RMS_EPS) * g


def _dot_t(a, b):
    return lax.dot_general(a, b, (((1,), (1,)), ((), ())), preferred_element_type=F32)


def _dot(a, b):
    return jnp.dot(a, b, preferred_element_type=F32)


def _lambda(lq1, lk1, lq2, lk2):
    return (jnp.exp(jnp.sum(lq1[...] * lk1[...], axis=1, keepdims=True))
            - jnp.exp(jnp.sum(lq2[...] * lk2[...], axis=1, keepdims=True)) + LAM_INIT)


def _inproj_kernel(x_ref, g_ref, w_ref, bf_ref, cos_ref, sa_ref, sb_ref,
                   fq16, fk32, fv32, dq16, dk32, dv32, lf_ref, *attn_copies):
    x = x_ref[0]
    h16 = _rms(x, g_ref[...]).astype(BF16)

    def seg(k, width=GROUP_W):
        return _dot(h16, w_ref[:, GROUP_W * k:GROUP_W * k + width])

    cos = jnp.concatenate([cos_ref[...]] * 4, axis=1)
    sa = jnp.concatenate([sa_ref[...]] * 4, axis=1)
    sb = jnp.concatenate([sb_ref[...]] * 4, axis=1)

    def rope(v):
        return (v * cos + pltpu.roll(v, GROUP_W - HEAD_DIM // 2, axis=1) * sa
                + pltpu.roll(v, HEAD_DIM // 2, axis=1) * sb)

    fq16[0] = (seg(0) * QK_SCALE).astype(BF16)
    fk = seg(1)
    fk32[0] = fk
    fv = seg(2)
    fv32[0] = fv
    dq16[0] = (rope(seg(3)) * QK_SCALE).astype(BF16)
    dk = rope(seg(4))
    dk32[0] = dk
    dv = seg(5)
    dv32[0] = dv
    ff = seg(6, LANES) + bf_ref[...]
    lf_ref[0] = (jnp.minimum(ff, 0.0) - jnp.log1p(jnp.exp(-jnp.abs(ff))))[:, :FOX_HEADS]
    if attn_copies:
        fk16, fvt16, dk16, dvt16 = attn_copies
        fk16[0] = fk.astype(BF16)
        dk16[0] = dk.astype(BF16)
        fvt16[0] = fv.T.astype(BF16)
        dvt16[0] = dv.T.astype(BF16)


def _inproj(x, g_attn, w_re, bf_pad, cos, sa, sb, rows_out, tm, attn_copies):
    n, rows_in, _ = x.shape
    grid = (n, rows_in // tm)
    row_blk = lambda w: pl.BlockSpec((1, tm, w), lambda b, j: (b, j, 0))
    col_blk = pl.BlockSpec((1, GROUP_W, tm), lambda b, j: (b, 0, j))
    const = lambda shape: pl.BlockSpec(shape, lambda b, j: (0,) * len(shape))
    tab = pl.BlockSpec((tm, LANES), lambda b, j: (j, 0))
    o16 = jax.ShapeDtypeStruct((n, rows_in, GROUP_W), BF16)
    o16t = jax.ShapeDtypeStruct((n, GROUP_W, rows_in), BF16)
    o32 = jax.ShapeDtypeStruct((n, rows_out, GROUP_W), F32)
    out_specs = [row_blk(GROUP_W)] * 6 + [row_blk(FOX_HEADS)]
    out_shape = [o16, o32, o32, o16, o32, o32, jax.ShapeDtypeStruct((n, rows_out, FOX_HEADS), F32)]
    if attn_copies:
        out_specs += [row_blk(GROUP_W), col_blk, row_blk(GROUP_W), col_blk]
        out_shape += [o16, o16t, o16, o16t]
    return pl.pallas_call(
        _inproj_kernel,
        grid=grid,
        in_specs=[row_blk(D_MODEL), const((1, D_MODEL)), const((D_MODEL, W_IN_COLS)),
                  const((1, LANES)), tab, tab, tab],
        out_specs=out_specs,
        out_shape=out_shape,
        compiler_params=_params(2),
        name="inproj",
    )(x, g_attn, w_re, bf_pad, cos, sa, sb)


def _cumsum_kernel(x_ref, o_ref):
    rows, cols = x_ref.shape
    lane = lax.broadcasted_iota(I32, (rows, LANES), 1)
    carry = jnp.zeros((rows, 1), F32)
    for c in range(cols // LANES):
        x = x_ref[:, LANES * c:LANES * (c + 1)]
        s = 1
        while s < LANES:
            x = x + jnp.where(lane >= s, pltpu.roll(x, s, axis=1), 0.0)
            s *= 2
        x = x + carry
        o_ref[:, LANES * c:LANES * (c + 1)] = x
        carry = x[:, LANES - 1:LANES]


def _cumsum_lanes(x):
    return pl.pallas_call(
        _cumsum_kernel,
        out_shape=jax.ShapeDtypeStruct(x.shape, F32),
        name="cumsum",
    )(x)


def _attn_kernel(fq_ref, dq_ref, fk_ref, fvt_ref, dk_ref, dvt_ref, cum_ref, cumt_ref,
                 lq1, lk1, lq2, lk2, gfox_ref, gdiff_ref, o_ref, ckb, qh, s_sc, p_sc, *accs):
    t = pl.program_id(1)
    lp = fk_ref.shape[1]

    @pl.when(t == 0)
    def _():
        for h in range(FOX_HEADS):
            ckb[h] = jnp.broadcast_to(cum_ref[0, :, h:h + 1], (lp, LANES))

    key = lax.broadcasted_iota(I32, (Q_TILE, Q_TILE), 0)
    qry = lax.broadcasted_iota(I32, (Q_TILE, Q_TILE), 1)
    lane_lo = qry < HEAD_DIM
    qstart = pl.multiple_of(t * Q_TILE, Q_TILE)

    chains = []
    for kind, (q_ref, k_ref, vt_ref) in enumerate(((fq_ref, fk_ref, fvt_ref), (dq_ref, dk_ref, dvt_ref))):
        for g in range(4):
            q = q_ref[0, :, LANES * g:LANES * (g + 1)].astype(F32)
            for half in range(2):
                c = 8 * kind + 2 * g + half
                keep = lane_lo if half == 0 else jnp.logical_not(lane_lo)
                qh[c] = jnp.where(keep, q, 0.0).astype(BF16)
                v_rows = (LANES * g + HEAD_DIM * half, HEAD_DIM) if kind == 0 else (LANES * g, LANES)
                accs[c][...] = jnp.zeros(accs[c].shape, F32)
                chains.append((c, kind, g, 2 * g + half, k_ref, vt_ref, v_rows))

    chain_row = lax.broadcasted_iota(I32, (N_CHAINS, Q_TILE), 0)

    def body(j, carry, diagonal=False):
        m_all, l_all = carry
        start = pl.multiple_of(j * Q_TILE, Q_TILE)
        for c, kind, g, h, k_ref, vt_ref, (r0, nr) in chains:
            k = k_ref[0, pl.ds(start, Q_TILE), LANES * g:LANES * (g + 1)]
            s_sc[c] = _dot_t(k, qh[c])
        alphas = []
        for c, kind, g, h, k_ref, vt_ref, (r0, nr) in chains:
            s = s_sc[c]
            if kind == 0:
                cq = cumt_ref[0, h:h + 1, pl.ds(qstart, Q_TILE)]
                s = s + (cq - ckb[h, pl.ds(start, Q_TILE), :])
            if diagonal:
                s = jnp.where(key <= qry, s, NEG_INF)
            m = m_all[c:c + 1]
            m_new = jnp.maximum(m, jnp.max(s, axis=0, keepdims=True))
            alpha = jnp.exp(m - m_new)
            p = jnp.exp(s - m_new)
            l_new = alpha * l_all[c:c + 1] + jnp.sum(p, axis=0, keepdims=True)
            m_all = jnp.where(chain_row == c, m_new, m_all)
            l_all = jnp.where(chain_row == c, l_new, l_all)
            p_sc[c] = p.astype(BF16)
            alphas.append(alpha)
        for c, kind, g, h, k_ref, vt_ref, (r0, nr) in chains:
            vt = vt_ref[0, r0:r0 + nr, pl.ds(start, Q_TILE)]
            accs[c][...] = alphas[c] * accs[c][...] + _dot(vt, p_sc[c])
        return m_all, l_all

    init = (jnp.full((N_CHAINS, Q_TILE), NEG_INF, F32), jnp.zeros((N_CHAINS, Q_TILE), F32))
    _, l_all = body(t, lax.fori_loop(0, t, body, init), diagonal=True)

    out = [accs[c][...] / l_all[c:c + 1] for c in range(N_CHAINS)]
    fox = [jnp.concatenate([out[2 * g], out[2 * g + 1]], axis=0).T for g in range(4)]
    o_f = _rms(jnp.concatenate(fox, axis=1), gfox_ref[...])
    lam = _lambda(lq1, lk1, lq2, lk2)
    diff = []
    for g in range(4):
        od = (out[8 + 2 * g] - lam * out[8 + 2 * g + 1]).T
        diff.append(_rms(od, gdiff_ref[...]) * (1.0 - LAM_INIT))
    o_ref[0] = jnp.concatenate([o_f] + diff, axis=1).astype(BF16)


def _attention(fq16, dq16, fk16, fvt16, dk16, dvt16, cum, cum_t, lams, g_fox, g_diff):
    n, lp, _ = fq16.shape
    grid = (n, lp // Q_TILE)
    tile = lambda w: pl.BlockSpec((1, Q_TILE, w), lambda b, t: (b, t, 0))
    full = lambda r, w: pl.BlockSpec((1, r, w), lambda b, t: (b, 0, 0))
    const = lambda shape: pl.BlockSpec(shape, lambda b, t: (0,) * len(shape))
    return pl.pallas_call(
        _attn_kernel,
        grid=grid,
        in_specs=[tile(GROUP_W), tile(GROUP_W), full(lp, GROUP_W), full(GROUP_W, lp),
                  full(lp, GROUP_W), full(GROUP_W, lp), full(lp, FOX_HEADS), full(FOX_HEADS, lp)]
                 + [const((1, HEAD_DIM))] * 4 + [const((1, GROUP_W)), const((1, LANES))],
        out_specs=tile(MIX_W),
        out_shape=jax.ShapeDtypeStruct((n, lp, MIX_W), BF16),
        scratch_shapes=[pltpu.VMEM((FOX_HEADS, lp, LANES), F32),
                        pltpu.VMEM((N_CHAINS, Q_TILE, LANES), BF16),
                        pltpu.VMEM((N_CHAINS, Q_TILE, Q_TILE), F32),
                        pltpu.VMEM((N_CHAINS, Q_TILE, Q_TILE), BF16)]
                       + [pltpu.VMEM((HEAD_DIM, Q_TILE), F32)] * 8
                       + [pltpu.VMEM((LANES, Q_TILE), F32)] * 8,
        compiler_params=_params(2),
        name="attention",
    )(fq16, dq16, fk16, fvt16, dk16, dvt16, cum, cum_t, *lams, g_fox, g_diff)


def _col_bcast(row):
    return jnp.concatenate(
        [jnp.broadcast_to(row[:, LANES * g:LANES * (g + 1)], (LANES, LANES)).T for g in range(4)],
        axis=0)


def _row_from_col(col):
    return jnp.concatenate(
        [jnp.broadcast_to(col[LANES * g:LANES * (g + 1)], (LANES, LANES)).T[0:1] for g in range(4)],
        axis=1)


def _decode_kernel(pt_ref, fq_ref, dq_ref, fkn_ref, fvn_ref, dkn_ref, dvn_ref, lfn_ref, *refs):
    pp = DEC_PAGES
    kt, vt, lft, dk, dv = (refs[i * pp:(i + 1) * pp] for i in range(5))
    (lq1, lk1, lq2, lk2, gfox_ref, gdiff_ref, o_ref,
     qb, mf, lf, af, md, ld, ad, carry) = refs[5 * pp:]
    p = pl.program_id(1)
    heads = FOX_HEADS

    def head_rows(row):
        return jnp.concatenate([row[:, LANES * (j // 2):LANES * (j // 2 + 1)] for j in range(heads)],
                               axis=0)

    sub = lax.broadcasted_iota(I32, (heads, LANES), 0)
    lane = lax.broadcasted_iota(I32, (heads, LANES), 1)
    qd = jnp.where((lane // HEAD_DIM) == (sub % 2), head_rows(dq_ref[0].astype(F32)), 0.0)

    @pl.when(p == 0)
    def _():
        fq = fq_ref[0].astype(F32)
        qb[...] = _col_bcast(fq)
        hrow = lax.broadcasted_iota(I32, (heads, GROUP_W), 0)
        hlane = lax.broadcasted_iota(I32, (heads, GROUP_W), 1)
        own = (hlane // HEAD_DIM) == hrow
        mf[...] = jnp.sum(jnp.where(own, fq * fkn_ref[0], 0.0), axis=1, keepdims=True)
        lf[...] = jnp.ones_like(lf)
        lane0 = lax.broadcasted_iota(I32, (GROUP_W, LANES), 1) == 0
        af[...] = jnp.where(lane0, _col_bcast(fvn_ref[0]), 0.0)
        md[...] = jnp.sum(qd * head_rows(dkn_ref[0]), axis=1, keepdims=True)
        ld[...] = jnp.ones_like(ld)
        ad[...] = head_rows(dvn_ref[0])
        carry[...] = lfn_ref[0]

    c = carry[...]
    scores = []
    for i in range(pp):
        x = lft[i][0]
        incl = x
        step = 1
        while step < PAGE:
            incl = incl + jnp.where(lane < PAGE - step, pltpu.roll(incl, PAGE - step, axis=1), 0.0)
            step *= 2
        bias = (incl - x) + c
        c = c + incl[:, 0:1]
        rows = [jnp.sum(kt[i][0, h] * qb[HEAD_DIM * h:HEAD_DIM * (h + 1), :], axis=0, keepdims=True)
                for h in range(heads)]
        scores.append(jnp.concatenate(rows, axis=0) + bias)
    carry[...] = c
    m_old = mf[...]
    m_new = m_old
    for s in scores:
        m_new = jnp.maximum(m_new, jnp.max(s, axis=1, keepdims=True))
    alpha = jnp.exp(m_old - m_new)
    probs = [jnp.exp(s - m_new) for s in scores]
    l_new = alpha * lf[...]
    for pr in probs:
        l_new = l_new + jnp.sum(pr, axis=1, keepdims=True)
    lf[...] = l_new
    mf[...] = m_new
    for h in range(heads):
        rows_h = slice(HEAD_DIM * h, HEAD_DIM * (h + 1))
        acc = af[rows_h, :] * jnp.broadcast_to(alpha[h:h + 1, :], (HEAD_DIM, LANES))
        for i in range(pp):
            acc = acc + jnp.broadcast_to(probs[i][h:h + 1, :], (HEAD_DIM, LANES)) * vt[i][0, h]
        af[rows_h, :] = acc

    qd16 = qd.astype(BF16)
    drow = lax.broadcasted_iota(I32, (heads, PAGE * 4), 0)
    dcol = lax.broadcasted_iota(I32, (heads, PAGE * 4), 1)
    visible = (dcol % 4) == (drow // 2)
    scores = [jnp.where(visible, _dot_t(qd16, dk[i][0].astype(BF16)), NEG_INF) for i in range(pp)]
    m_old = md[...]
    m_new = m_old
    for s in scores:
        m_new = jnp.maximum(m_new, jnp.max(s, axis=1, keepdims=True))
    alpha = jnp.exp(m_old - m_new)
    l_new = alpha * ld[...]
    acc = alpha * ad[...]
    for i in range(pp):
        pr = jnp.exp(scores[i] - m_new)
        l_new = l_new + jnp.sum(pr, axis=1, keepdims=True)
        acc = acc + _dot(pr.astype(BF16), dv[i][0].astype(BF16))
    ld[...] = l_new
    md[...] = m_new
    ad[...] = acc

    @pl.when(p == pl.num_programs(1) - 1)
    def _():
        o_col = jnp.sum(af[...], axis=1, keepdims=True)
        l_col = jnp.concatenate([jnp.broadcast_to(lf[h:h + 1, :], (HEAD_DIM, 1)) for h in range(heads)],
                                axis=0)
        o_f = _rms(_row_from_col(o_col / l_col), gfox_ref[...])
        lam = _lambda(lq1, lk1, lq2, lk2)
        od = ad[...] / ld[...]
        parts = [_rms(od[2 * h:2 * h + 1] - lam * od[2 * h + 1:2 * h + 2], gdiff_ref[...])
                 * (1.0 - LAM_INIT) for h in range(4)]
        o_ref[0] = jnp.concatenate([o_f] + parts, axis=1).astype(BF16)


def _decode(page_table, fq16, dq16, fk_new, fv_new, dk_new, dv_new, lf_col,
            cache_kt, cache_vt, cache_lf_t, cache_dk, cache_dv, lams, g_fox, g_diff):
    nb, n_pages = page_table.shape
    pp = DEC_PAGES
    tok = lambda w: pl.BlockSpec((1, 1, w), lambda b, p, pt: (b, 0, 0))
    const = lambda shape: pl.BlockSpec(shape, lambda b, p, pt: (0,) * len(shape))

    def pages(block):
        zeros = (0,) * (len(block) - 1)
        return [pl.BlockSpec(block, lambda b, p, pt, i=i: (pt[b, n_pages - 1 - (p * pp + i)],) + zeros)
                for i in range(pp)]

    grid_spec = pltpu.PrefetchScalarGridSpec(
        num_scalar_prefetch=1,
        grid=(nb, n_pages // pp),
        in_specs=[tok(GROUP_W)] * 6 + [pl.BlockSpec((1, FOX_HEADS, 1), lambda b, p, pt: (b, 0, 0))]
                 + pages((1, FOX_HEADS, HEAD_DIM, PAGE)) + pages((1, FOX_HEADS, HEAD_DIM, PAGE))
                 + pages((1, FOX_HEADS, PAGE)) + pages((1, PAGE * 4, LANES)) + pages((1, PAGE * 4, LANES))
                 + [const((1, HEAD_DIM))] * 4 + [const((1, GROUP_W)), const((1, LANES))],
        out_specs=pl.BlockSpec((1, 1, MIX_W), lambda b, p, pt: (b, 0, 0)),
        scratch_shapes=[pltpu.VMEM((GROUP_W, LANES), F32),
                        pltpu.VMEM((FOX_HEADS, 1), F32), pltpu.VMEM((FOX_HEADS, 1), F32),
                        pltpu.VMEM((GROUP_W, LANES), F32),
                        pltpu.VMEM((FOX_HEADS, 1), F32), pltpu.VMEM((FOX_HEADS, 1), F32),
                        pltpu.VMEM((FOX_HEADS, LANES), F32),
                        pltpu.VMEM((FOX_HEADS, 1), F32)],
    )
    return pl.pallas_call(
        _decode_kernel,
        grid_spec=grid_spec,
        out_shape=jax.ShapeDtypeStruct((nb, 1, MIX_W), BF16),
        compiler_params=_params(2),
        name="decode",
    )(page_table, fq16, dq16, fk_new, fv_new, dk_new, dv_new, lf_col,
      *([cache_kt] * pp), *([cache_vt] * pp), *([cache_lf_t] * pp),
      *([cache_dk] * pp), *([cache_dv] * pp), *lams, g_fox, g_diff)


def _topk_rows(s, order, payload=None):
    big = jnp.int32(2 ** 30)
    vals, picks = [], []
    for _ in range(PEER_TOPK):
        m = jnp.max(s, axis=0, keepdims=True)
        r = jnp.min(jnp.where(s == m, order, big), axis=0, keepdims=True)
        hit = order == r
        vals.append(m)
        if payload is None:
            picks.append(r)
        else:
            picks.append(jnp.sum(jnp.where(hit, payload, 0), axis=0, keepdims=True))
        s = jnp.where(hit, NEG_INF, s)
    return jnp.concatenate(vals, axis=0), jnp.concatenate(picks, axis=0)


def _mix_kernel(x_ref, mixed_ref, wo_ref, g_ref, wpq_ref, k1_ref, k2_ref,
                x2_ref, h2_ref, idx_ref, gate_ref):
    tm = x_ref.shape[1]
    x2 = x_ref[0] + _dot(mixed_ref[0], wo_ref[...])
    x2_ref[0] = x2
    h16 = _rms(x2, g_ref[...]).astype(BF16)
    h2_ref[0] = h16.astype(F32)
    key_rank = lax.broadcasted_iota(I32, (PEER_KEYS, tm), 0)
    n_b = [PEER_TOPK // (a + 1) for a in range(PEER_TOPK)]
    rank16 = lax.broadcasted_iota(I32, (PEER_TOPK, tm), 0)
    cand_rank = jnp.concatenate([rank16[:n_b[a]] + PEER_TOPK * a for a in range(PEER_TOPK)], axis=0)
    for hd in range(PEER_HEADS):
        q = _dot(h16, wpq_ref[:, 2 * LANES * hd:2 * LANES * (hd + 1)])
        s1 = _dot_t(k1_ref[...], q[:, :LANES].astype(BF16))
        s2 = _dot_t(k2_ref[...], q[:, LANES:].astype(BF16))
        v1, i1 = _topk_rows(s1, key_rank)
        v2, i2 = _topk_rows(s2, key_rank)
        cand = jnp.concatenate([v1[a:a + 1] + v2[:n_b[a]] for a in range(PEER_TOPK)], axis=0)
        cidx = jnp.concatenate([i1[a:a + 1] * PEER_KEYS + i2[:n_b[a]] for a in range(PEER_TOPK)],
                               axis=0)
        sc, idx = _topk_rows(cand, cand_rank, cidx)
        e = jnp.exp(sc - sc[0:1])
        gate = e / jnp.sum(e, axis=0, keepdims=True)
        idx_ref[0, PEER_TOPK * hd:PEER_TOPK * (hd + 1), :] = idx * HALF_ROWS
        gate_ref[0, PEER_TOPK * hd:PEER_TOPK * (hd + 1), :] = gate


def _mix(x, mixed, w_o16, g_ffn, w_pq16, k1_16, k2_16, tm):
    n, rows, _ = x.shape
    grid = (n, rows // tm)
    row_blk = lambda w: pl.BlockSpec((1, tm, w), lambda b, j: (b, j, 0))
    col_blk = pl.BlockSpec((1, PEER_PICKS, tm), lambda b, j: (b, 0, j))
    const = lambda shape: pl.BlockSpec(shape, lambda b, j: (0,) * len(shape))
    return pl.pallas_call(
        _mix_kernel,
        grid=grid,
        in_specs=[row_blk(D_MODEL), row_blk(MIX_W), const((MIX_W, D_MODEL)), const((1, D_MODEL)),
                  const((D_MODEL, 2 * LANES * PEER_HEADS)), const((PEER_KEYS, LANES)),
                  const((PEER_KEYS, LANES))],
        out_specs=[row_blk(D_MODEL), row_blk(D_MODEL), col_blk, col_blk],
        out_shape=[jax.ShapeDtypeStruct((n, rows, D_MODEL), F32),
                   jax.ShapeDtypeStruct((n, rows, D_MODEL), F32),
                   jax.ShapeDtypeStruct((n, PEER_PICKS, rows), I32),
                   jax.ShapeDtypeStruct((n, PEER_PICKS, rows), F32)],
        compiler_params=_params(2),
        name="mix",
    )(x, mixed, w_o16, g_ffn, w_pq16, k1_16, k2_16)


HI_MASK = -65536


def _red_stride(tb):
    return tb * PEER_PICKS + 8


def _unpack(word):
    hi = pltpu.bitcast(word & HI_MASK, F32)
    lo = pltpu.bitcast(word << 16, F32)
    return hi, lo


def _expert_row(tab_ref, row):
    return tab_ref[pl.ds(pl.multiple_of(row, HALF_ROWS), HALF_ROWS), :]


def _peer_u_kernel(idx_ref, x_ref, gate_ref, tab_ref, w_ref, red, act):
    tb = gate_ref.shape[0]
    stride = _red_stride(tb)

    def gather(t, _):
        row = x_ref[pl.ds(t, 1), :]
        chunk = lambda r: row[:, LANES * r:LANES * (r + 1)]
        xa = jnp.concatenate([chunk(r) for r in range(HALF_ROWS)], axis=0)
        xb = jnp.concatenate([chunk(HALF_ROWS + r) for r in range(HALF_ROWS)], axis=0)
        picks = idx_ref.at[t]
        base = pl.multiple_of(t * PEER_PICKS, PEER_PICKS)
        window = red.at[pl.ds(base, (HALF_ROWS - 1) * stride + PEER_PICKS)]
        for j in range(PEER_PICKS):
            hi, lo = _unpack(_expert_row(tab_ref, picks[j]))
            window[pl.ds(j, HALF_ROWS, stride=stride), :] = hi * xa + lo * xb
        return 0

    def reduce(t, _):
        base = pl.multiple_of(t * PEER_PICKS, PEER_PICKS)
        blk = lambda s: red[pl.ds(base + s * stride, PEER_PICKS), :]
        part = (blk(0) + blk(1)) + (blk(2) + blk(3))
        act[pl.ds(t, 1), :] = jnp.sum(part.T, axis=0, keepdims=True)
        return 0

    gather(0, 0)

    def step(t, _):
        reduce(t - 1, 0)
        gather(t, 0)
        return 0

    lax.fori_loop(1, tb, step, 0)
    reduce(tb - 1, 0)
    a = act[...]
    w = gate_ref[...] * (0.5 * a * (1.0 + lax.erf(a * (2.0 ** -0.5))))
    w_ref[...] = w.astype(BF16).astype(F32)


def _peer_v_kernel(idx_ref, w_ref, tab_ref, x2_ref, g_ref, y_ref, peer):
    tb = idx_ref.shape[0]
    n_acc = 2

    def spread(t):
        return jnp.broadcast_to(w_ref[pl.ds(t, 1), :], (PEER_PICKS, LANES)).T

    def token(t, wt):
        wt_next = spread(jnp.minimum(t + 1, tb - 1))
        acc_a = [jnp.zeros((HALF_ROWS, LANES), F32) for _ in range(n_acc)]
        acc_b = [jnp.zeros((HALF_ROWS, LANES), F32) for _ in range(n_acc)]
        picks = idx_ref.at[t]
        for j in range(PEER_PICKS):
            hi, lo = _unpack(_expert_row(tab_ref, picks[j]))
            w = jnp.broadcast_to(wt[j:j + 1, :], (HALF_ROWS, LANES))
            acc_a[j % n_acc] = acc_a[j % n_acc] + w * hi
            acc_b[j % n_acc] = acc_b[j % n_acc] + w * lo
        a = functools.reduce(lambda u, v: u + v, acc_a)
        b = functools.reduce(lambda u, v: u + v, acc_b)
        peer[pl.ds(t, 1), :] = jnp.concatenate(
            [a[r:r + 1] for r in range(HALF_ROWS)] + [b[r:r + 1] for r in range(HALF_ROWS)], axis=1)
        return wt_next

    lax.fori_loop(0, tb, token, spread(0))
    y_ref[...] = _rms(x2_ref[...] + peer[...], g_ref[...])


def _resident(shape):
    return pl.BlockSpec(shape, lambda *_: (0,) * len(shape), pipeline_mode=pl.Buffered(1))


def _peer_u(idx, h2, gate, table, tb):
    n, rows, _ = idx.shape
    grid = (n, rows // tb)
    return pl.pallas_call(
        _peer_u_kernel,
        grid=grid,
        in_specs=[pl.BlockSpec((None, tb, PEER_PICKS), lambda b, j: (b, j, 0),
                               memory_space=pltpu.SMEM),
                  pl.BlockSpec((None, tb, D_MODEL), lambda b, j: (b, j, 0)),
                  pl.BlockSpec((None, tb, PEER_PICKS), lambda b, j: (b, j, 0)),
                  _resident(table.shape)],
        out_specs=pl.BlockSpec((None, tb, PEER_PICKS), lambda b, j: (b, j, 0)),
        out_shape=jax.ShapeDtypeStruct((n, rows, PEER_PICKS), F32),
        scratch_shapes=[pltpu.VMEM((HALF_ROWS * _red_stride(tb), LANES), F32),
                        pltpu.VMEM((tb, PEER_PICKS), F32)],
        compiler_params=_params(2),
        name="peer_u",
    )(idx, h2, gate, table)


def _peer_v(idx, w, table, x2, g_final, tb):
    n, rows, _ = idx.shape
    grid = (n, rows // tb)
    row_blk = lambda width: pl.BlockSpec((None, tb, width), lambda b, j: (b, j, 0))
    return pl.pallas_call(
        _peer_v_kernel,
        grid=grid,
        in_specs=[pl.BlockSpec((None, tb, PEER_PICKS), lambda b, j: (b, j, 0),
                               memory_space=pltpu.SMEM),
                  row_blk(PEER_PICKS), _resident(table.shape), row_blk(D_MODEL),
                  pl.BlockSpec((1, D_MODEL), lambda b, j: (0, 0))],
        out_specs=row_blk(D_MODEL),
        out_shape=jax.ShapeDtypeStruct((n, rows, D_MODEL), F32),
        scratch_shapes=[pltpu.VMEM((tb, D_MODEL), F32)],
        compiler_params=_params(2),
        name="peer_v",
    )(idx, w, table, x2, g_final)


def _pack_table(t):
    t16 = t.astype(BF16)
    hi = lax.bitcast_convert_type(t16[:, :GROUP_W], jnp.uint16).astype(jnp.uint32)
    lo = lax.bitcast_convert_type(t16[:, GROUP_W:], jnp.uint16).astype(jnp.uint32)
    packed = lax.bitcast_convert_type((hi << 16) | lo, I32)
    return packed.reshape(t.shape[0] * HALF_ROWS, LANES)


def _rope_tables(pos):
    half = HEAD_DIM // 2
    inv = ROPE_THETA ** (-jnp.arange(half, dtype=F32) / half)
    ang = pos.astype(F32)[:, None] * inv[None, :]
    cos, sin = jnp.cos(ang), jnp.sin(ang)
    zero = jnp.zeros_like(sin)
    return (jnp.concatenate([cos, cos, cos, cos], axis=1),
            jnp.concatenate([-sin, zero, -sin, zero], axis=1),
            jnp.concatenate([zero, sin, zero, sin], axis=1))


def kernel(x_prompt, x_sample, cache_fox_k, cache_fox_v, cache_fox_logf, cache_diff_k, cache_diff_v, page_table, meta_tokens, g_attn, w_in, b_f, lam_q1, lam_k1, lam_q2, lam_k2, g_fox, g_diff, w_o, g_ffn, w_pq, sub_k1, sub_k2, expert_u, expert_v, g_final):
    assert w_in.shape[0] == 1, "single-layer trunk"
    nb, seq, _ = x_prompt.shape
    db = x_sample.shape[0]
    n_pool = cache_fox_k.shape[1]
    n_pages = page_table.shape[1]
    lp_true = seq + N_META
    lp = -(-lp_true // Q_TILE) * Q_TILE
    peer_tb = 48
    peer_v_tb = 344

    w = w_in[0]
    c = [0, 512, 1024, 1536, 1544, 2056, 2568, 3080]
    w_re = jnp.concatenate([w[:, c[0]:c[3]], w[:, c[4]:c[7]], w[:, c[3]:c[4]],
                            jnp.zeros((D_MODEL, LANES - FOX_HEADS), F32)], axis=1).astype(BF16)
    bf_pad = jnp.concatenate([b_f[0], jnp.zeros((LANES - FOX_HEADS,), F32)])[None]
    g_attn2, g_ffn2, g_fox2, g_diff2 = g_attn[0][None], g_ffn[0][None], g_fox[0][None], g_diff[0][None]
    lams = (lam_q1[0][None], lam_k1[0][None], lam_q2[0][None], lam_k2[0][None])
    w_o16 = w_o[0].astype(BF16)
    w_pq16 = w_pq[0].astype(BF16)
    k1_16 = sub_k1[0].astype(BF16)
    k2_16 = sub_k2[0].astype(BF16)
    u_tab = _pack_table(expert_u[0])
    v_tab = _pack_table(expert_v[0])
    g_fin = g_final[None]

    xp = jnp.concatenate([jnp.broadcast_to(meta_tokens[None], (nb, N_META, D_MODEL)), x_prompt,
                          jnp.zeros((nb, lp - lp_true, D_MODEL), F32)], axis=1)
    cos, sa, sb = _rope_tables(jnp.arange(lp))
    (fq16, fk32, fv32, dq16, dk32, dv32, lf, fk16, fvt16, dk16, dvt16) = _inproj(
        xp, g_attn2, w_re, bf_pad, cos, sa, sb, lp_true, Q_TILE, True)

    lf_t = jnp.pad(jnp.swapaxes(lf, 1, 2), ((0, 0), (0, 0), (0, lp - lp_true)))
    cum_t = _cumsum_lanes(lf_t.reshape(nb * FOX_HEADS, lp)).reshape(nb, FOX_HEADS, lp)
    cum = jnp.swapaxes(cum_t, 1, 2)
    mixed = _attention(fq16, dq16, fk16, fvt16, dk16, dvt16, cum, cum_t, lams, g_fox2, g_diff2)

    x2, h2, idx_t, gate_t = _mix(xp, mixed, w_o16, g_ffn2, w_pq16, k1_16, k2_16, Q_TILE)
    idx = jnp.swapaxes(idx_t, 1, 2)[:, :lp_true]
    gate = jnp.swapaxes(gate_t, 1, 2)[:, :lp_true]
    wts = _peer_u(idx, h2, gate, u_tab, peer_tb)
    y_prompt = _peer_v(idx, wts, v_tab, x2, g_fin, peer_v_tb)[:, N_META:]

    cos_s, sa_s, sb_s = _rope_tables(jnp.full((db,), n_pages * PAGE))
    xs = x_sample.reshape(1, db, D_MODEL)
    (sfq16, sfk32, sfv32, sdq16, sdk32, sdv32, slf) = _inproj(
        xs, g_attn2, w_re, bf_pad, cos_s, sa_s, sb_s, db, db, False)
    tok = lambda a: a.reshape(db, 1, a.shape[-1])
    cache_kt = jnp.transpose(cache_fox_k[0], (0, 2, 3, 1))
    cache_vt = jnp.transpose(cache_fox_v[0], (0, 2, 3, 1))
    cache_lf_t = jnp.swapaxes(cache_fox_logf[0], 1, 2)
    cache_dk = cache_diff_k[0].reshape(n_pool, PAGE * 4, LANES)
    cache_dv = cache_diff_v[0].reshape(n_pool, PAGE * 4, LANES)
    mixed_s = _decode(page_table, tok(sfq16), tok(sdq16), tok(sfk32), tok(sfv32), tok(sdk32),
                      tok(sdv32), slf.reshape(db, FOX_HEADS, 1),
                      cache_kt, cache_vt, cache_lf_t, cache_dk, cache_dv, lams, g_fox2, g_diff2)
    sx2, sh2, sidx_t, sgate_t = _mix(xs, mixed_s.reshape(1, db, MIX_W), w_o16, g_ffn2, w_pq16,
                                     k1_16, k2_16, db)
    sidx = jnp.swapaxes(sidx_t, 1, 2)
    sgate = jnp.swapaxes(sgate_t, 1, 2)
    swts = _peer_u(sidx, sh2, sgate, u_tab, db)
    y_sample = _peer_v(sidx, swts, v_tab, sx2, g_fin, db).reshape(db, 1, D_MODEL)

    heads = lambda a, h: a.reshape((1, a.shape[0], a.shape[1], h, a.shape[2] // h))
    sheads = lambda a, h: a.reshape((1, db, 1, h, a.shape[-1] // h))
    return (y_prompt, y_sample,
            heads(fk32, 8), heads(fv32, 8), lf[None], heads(dk32, 4), heads(dv32, 4),
            sheads(sfk32, 8), sheads(sfv32, 8), slf.reshape(1, db, 1, FOX_HEADS),
            sheads(sdk32, 4), sheads(sdv32, 4))
```
